```python
import math
import jax, jax.numpy as jnp
from jax import lax
import numpy as np

D_MODEL = 1024
BATCH = 2
SEQ = 8192
DEPTH = 2
DEC_BATCH = 32
DEC_SEQ = 1
PAST_LEN = 8192
PAGE_SIZE = 128

BRANCH_W = D_MODEL // 2
A_HEADS = 4
A_DH = BRANCH_W // (2 * A_HEADS)
A_DV = 2 * A_DH
R_HEADS = 4
R_DK = BRANCH_W // R_HEADS
R_DV = BRANCH_W // R_HEADS
R_CHUNK = 128
S_GROUP = 16
S_GROUPS = BRANCH_W // S_GROUP
S_STATE = 64
N_BRANCH = 3
N_IN = 10 * BRANCH_W + N_BRANCH * D_MODEL
Q_BLOCK = 128
ROPE_THETA = 10000.0
EPS = 1e-6

kernel_name = 'hybrid_diffattn_retnet_s5_step'


def _rmsnorm(x, g):
    xf = x.astype(jnp.float32)
    y = xf * lax.rsqrt(jnp.mean(xf * xf, axis=-1, keepdims=True) + EPS)
    return (y * g.astype(jnp.float32)).astype(x.dtype)


def _rotary(x, pos):
    d = x.shape[-1]
    half = d // 2
    inv = ROPE_THETA ** (-jnp.arange(half, dtype=jnp.float32) / half)
    ang = pos.astype(jnp.float32)[:, None] * inv[None, :]
    shape = (1, x.shape[1]) + (1,) * (x.ndim - 3) + (half,)
    cos = jnp.cos(ang).reshape(shape)
    sin = jnp.sin(ang).reshape(shape)
    xf = x.astype(jnp.float32)
    x1, x2 = xf[..., :half], xf[..., half:]
    return jnp.concatenate([x1 * cos - x2 * sin, x1 * sin + x2 * cos], axis=-1).astype(x.dtype)


def _diff_block(qb, qpos, k, v, kpos, lam):
    s = jnp.einsum('bqhcd,bkhcd->bhcqk', qb, k) * (A_DH ** -0.5)
    mask = kpos[None, :] <= qpos[:, None]
    s = jnp.where(mask[None, None, None], s, -jnp.inf)
    p = jax.nn.softmax(s, axis=-1)
    pd = p[:, :, 0] - lam * p[:, :, 1]
    return jnp.einsum('bhqk,bkhe->bqhe', pd, v)


def _diff_attention(q, k_new, v_new, k_past, v_past, pos, lam):
    f32 = jnp.float32
    if k_past is None:
        k, v, kpos = k_new, v_new, pos
    else:
        k = jnp.concatenate([k_past.astype(f32), k_new.astype(f32)], axis=1)
        v = jnp.concatenate([v_past.astype(f32), v_new.astype(f32)], axis=1)
        kpos = jnp.arange(k.shape[1])
    q, k, v = q.astype(f32), k.astype(f32), v.astype(f32)
    B, Lq, H = q.shape[0], q.shape[1], q.shape[2]
    if Lq % Q_BLOCK == 0:
        nb = Lq // Q_BLOCK
        qb = q.reshape(B, nb, Q_BLOCK, H, 2, A_DH).transpose(1, 0, 2, 3, 4, 5)
        pb = pos.reshape(nb, Q_BLOCK)
        o = lax.map(lambda a: _diff_block(a[0], a[1], k, v, kpos, lam), (qb, pb))
        o = o.transpose(1, 0, 2, 3, 4).reshape(B, Lq, H, A_DV)
    else:
        o = _diff_block(q, pos, k, v, kpos, lam)
    return o


def _retention(q, k, v, s0):
    f32 = jnp.float32
    B, L, H, _ = q.shape
    dv = v.shape[-1]
    C = R_CHUNK if L % R_CHUNK == 0 else L
    n = L // C
    log_g = jnp.log1p(-jnp.exp2(-5.0 - jnp.arange(H, dtype=f32)))
    i = jnp.arange(C, dtype=f32)
    diff = i[:, None] - i[None, :]
    dmask = jnp.exp(jnp.where(diff >= 0, diff, jnp.inf)[None] * log_g[:, None, None])
    q_dec = jnp.exp((i + 1.0)[None, :] * log_g[:, None]).T[None, :, :, None]
    k_dec = jnp.exp((C - 1.0 - i)[None, :] * log_g[:, None]).T[None, :, :, None]
    c_dec = jnp.exp(C * log_g)[None, :, None, None]

    def to_chunks(t):
        return t.astype(f32).reshape(B, n, C, H, t.shape[-1]).transpose(1, 0, 2, 3, 4)

    def step(S, inp):
        qc, kc, vc = inp
        att = jnp.einsum('bihd,bjhd->bhij', qc, kc) * dmask
        o = jnp.einsum('bhij,bjhe->bihe', att, vc) + jnp.einsum('bihd,bhde->bihe', qc * q_dec, S)
        S = c_dec * S + jnp.einsum('bjhd,bjhe->bhde', kc * k_dec, vc)
        return S, o

    S, o = lax.scan(step, s0.astype(f32), (to_chunks(q), to_chunks(k), to_chunks(v)))
    return o.transpose(1, 0, 2, 3, 4).reshape(B, L, H, dv), S


def _s5(u, h0_re, h0_im, lam_re, lam_im, log_dt, b_re, b_im, c_re, c_im, d_skip):
    f32 = jnp.float32
    dt = jnp.exp(log_dt.astype(f32))[:, None]
    lre, lim = lam_re.astype(f32), lam_im.astype(f32)
    mag = jnp.exp(lre * dt)
    ab_re, ab_im = mag * jnp.cos(lim * dt), mag * jnp.sin(lim * dt)
    den = lre * lre + lim * lim
    nre = ab_re - 1.0
    cre = (nre * lre + ab_im * lim) / den
    cim = (ab_im * lre - nre * lim) / den
    b_re, b_im = b_re.astype(f32), b_im.astype(f32)
    bb_re = cre[..., None] * b_re - cim[..., None] * b_im
    bb_im = cre[..., None] * b_im + cim[..., None] * b_re
    bu_re = jnp.einsum('blgc,gpc->lbgp', u, bb_re)
    bu_im = jnp.einsum('blgc,gpc->lbgp', u, bb_im)
    L = u.shape[1]
    a_re = jnp.broadcast_to(ab_re[None, None], (L, 1) + ab_re.shape)
    a_im = jnp.broadcast_to(ab_im[None, None], (L, 1) + ab_im.shape)

    def combine(e1, e2):
        ar1, ai1, br1, bi1 = e1
        ar2, ai2, br2, bi2 = e2
        return (ar1 * ar2 - ai1 * ai2, ar1 * ai2 + ai1 * ar2,
                ar2 * br1 - ai2 * bi1 + br2, ar2 * bi1 + ai2 * br1 + bi2)

    acum_re, acum_im, h_re, h_im = lax.associative_scan(combine, (a_re, a_im, bu_re, bu_im), axis=0)
    h0r = h0_re.astype(f32)[None]
    h0i = h0_im.astype(f32)[None]
    h_re = h_re + acum_re * h0r - acum_im * h0i
    h_im = h_im + acum_re * h0i + acum_im * h0r
    y = (jnp.einsum('lbgp,gcp->blgc', h_re, c_re.astype(f32))
         - jnp.einsum('lbgp,gcp->blgc', h_im, c_im.astype(f32))
         + d_skip.astype(f32) * u)
    return y, h_re[-1], h_im[-1]


def _layer(x, pos, k_past, v_past, ret_s0, s5_re0, s5_im0, p, lam_init):
    f32 = jnp.float32
    B, L, _ = x.shape
    h = _rmsnorm(x, p['norm_gain'])
    z = h @ p['w_in']
    aq, ak, av, ag, rq, rk, rv, rg, su, sg, mg = jnp.split(z, [BRANCH_W * j for j in range(1, 11)], axis=-1)

    q = _rotary(_rmsnorm(aq.reshape(B, L, A_HEADS, 2, A_DH), p['q_norm_gain']), pos)
    k = _rotary(_rmsnorm(ak.reshape(B, L, A_HEADS, 2, A_DH), p['k_norm_gain']), pos)
    v = av.reshape(B, L, A_HEADS, A_DV)
    lam = (jnp.exp(jnp.sum(p['lam_q1'].astype(f32) * p['lam_k1'].astype(f32)))
           - jnp.exp(jnp.sum(p['lam_q2'].astype(f32) * p['lam_k2'].astype(f32))) + lam_init)
    o_a = _diff_attention(q, k, v, k_past, v_past, pos, lam)
    o_a = _rmsnorm(o_a, p['attn_out_gain']) * (1.0 - lam_init)
    y_attn = (o_a.reshape(B, L, BRANCH_W).astype(x.dtype) * jax.nn.silu(ag)) @ p['w_br_attn']

    rq_ = _rotary(rq.reshape(B, L, R_HEADS, R_DK), pos)
    rk_ = _rotary(rk.reshape(B, L, R_HEADS, R_DK), pos) * (R_DK ** -0.5)
    o_r, ret_state = _retention(rq_, rk_, rv.reshape(B, L, R_HEADS, R_DV), ret_s0)
    o_r = _rmsnorm(o_r, p['ret_out_gain'])
    y_ret = (o_r.reshape(B, L, BRANCH_W).astype(x.dtype) * jax.nn.silu(rg)) @ p['w_br_ret']

    y_s, s_re, s_im = _s5(su.reshape(B, L, S_GROUPS, S_GROUP).astype(f32), s5_re0, s5_im0,
                          p['s5_lam_re'], p['s5_lam_im'], p['s5_log_dt'], p['s5_b_re'], p['s5_b_im'],
                          p['s5_c_re'], p['s5_c_im'], p['s5_d'])
    y_s = jax.nn.gelu(y_s.reshape(B, L, BRANCH_W))
    ga, gb = jnp.split(y_s @ p['w_glu'].astype(f32) + p['b_glu'].astype(f32), 2, axis=-1)
    y_s5 = ((ga * jax.nn.sigmoid(gb)).astype(x.dtype) * jax.nn.silu(sg)) @ p['w_br_s5']

    gates = jax.nn.sigmoid(mg.reshape(B, L, N_BRANCH, D_MODEL))
    merged = gates[:, :, 0] * y_attn + gates[:, :, 1] * y_ret + gates[:, :, 2] * y_s5
    out = x + (merged @ p['w_out']).astype(x.dtype)
    return out, k.reshape(B, L, A_HEADS, A_DV), v, ret_state, s_re, s_im


def setup_inputs(seed: int = 0) -> dict:
    key = jax.random.key(seed)
    ks = jax.random.split(key, 40)
    f32 = jnp.float32

    def nrm(i, shape, scale):
        return scale * jax.random.normal(ks[i], shape, f32)

    n_pages = PAST_LEN // PAGE_SIZE
    used = DEC_BATCH * n_pages
    n_pool = used + max(1, used // 4)
    page_table = jax.random.permutation(ks[0], n_pool)[:used].reshape(DEC_BATCH, n_pages).astype(jnp.int32)
    n_idx = jnp.arange(S_STATE, dtype=f32)
    u = jax.random.uniform(ks[20], (DEPTH, S_GROUPS), f32)
    log_dt = math.log(1e-3) + u * (math.log(1e-1) - math.log(1e-3))
    return {
        'x_prompt': nrm(1, (BATCH, SEQ, D_MODEL), 1.0),
        'x_sample': nrm(2, (DEC_BATCH, DEC_SEQ, D_MODEL), 1.0),
        'cache_k': nrm(3, (DEPTH, n_pool, PAGE_SIZE, A_HEADS, A_DV), 1.0),
        'cache_v': nrm(4, (DEPTH, n_pool, PAGE_SIZE, A_HEADS, A_DV), 1.0),
        'state_ret': nrm(5, (DEPTH, DEC_BATCH, R_HEADS, R_DK, R_DV), 1.0),
        'state_s5_re': nrm(6, (DEPTH, DEC_BATCH, S_GROUPS, S_STATE), 1.0),
        'state_s5_im': nrm(7, (DEPTH, DEC_BATCH, S_GROUPS, S_STATE), 1.0),
        'page_table': page_table,
        'norm_gain': 1.0 + nrm(8, (DEPTH, D_MODEL), 0.01),
        'w_in': nrm(9, (DEPTH, D_MODEL, N_IN), D_MODEL ** -0.5),
        'q_norm_gain': 1.0 + nrm(10, (DEPTH, A_DH), 0.01),
        'k_norm_gain': 1.0 + nrm(11, (DEPTH, A_DH), 0.01),
        'lam_q1': nrm(12, (DEPTH, A_DH), 0.1),
        'lam_k1': nrm(13, (DEPTH, A_DH), 0.1),
        'lam_q2': nrm(14, (DEPTH, A_DH), 0.1),
        'lam_k2': nrm(15, (DEPTH, A_DH), 0.1),
        'attn_out_gain': 1.0 + nrm(16, (DEPTH, A_HEADS, A_DV), 0.01),
        'ret_out_gain': 1.0 + nrm(17, (DEPTH, R_HEADS, R_DV), 0.01),
        's5_lam_re': -0.5 + nrm(18, (DEPTH, S_GROUPS, S_STATE), 0.01),
        's5_lam_im': math.pi * n_idx + nrm(19, (DEPTH, S_GROUPS, S_STATE), 0.01),
        's5_log_dt': log_dt,
        's5_b_re': nrm(21, (DEPTH, S_GROUPS, S_STATE, S_GROUP), (2 * S_GROUP) ** -0.5),
        's5_b_im': nrm(22, (DEPTH, S_GROUPS, S_STATE, S_GROUP), (2 * S_GROUP) ** -0.5),
        's5_c_re': nrm(23, (DEPTH, S_GROUPS, S_GROUP, S_STATE), (2 * S_STATE) ** -0.5),
        's5_c_im': nrm(24, (DEPTH, S_GROUPS, S_GROUP, S_STATE), (2 * S_STATE) ** -0.5),
        's5_d': nrm(25, (DEPTH, S_GROUPS, S_GROUP), 1.0),
        'w_glu': nrm(26, (DEPTH, BRANCH_W, 2 * BRANCH_W), BRANCH_W ** -0.5),
        'b_glu': nrm(27, (DEPTH, 2 * BRANCH_W), 0.01),
        'w_br_attn': nrm(28, (DEPTH, BRANCH_W, D_MODEL), BRANCH_W ** -0.5),
        'w_br_ret': nrm(29, (DEPTH, BRANCH_W, D_MODEL), BRANCH_W ** -0.5),
        'w_br_s5': nrm(30, (DEPTH, BRANCH_W, D_MODEL), BRANCH_W ** -0.5),
        'w_out': nrm(31, (DEPTH, D_MODEL, D_MODEL), D_MODEL ** -0.5),
    }


def reference(x_prompt, x_sample, cache_k, cache_v, state_ret, state_s5_re, state_s5_im, page_table,
              norm_gain, w_in, q_norm_gain, k_norm_gain, lam_q1, lam_k1, lam_q2, lam_k2,
              attn_out_gain, ret_out_gain, s5_lam_re, s5_lam_im, s5_log_dt, s5_b_re, s5_b_im,
              s5_c_re, s5_c_im, s5_d, w_glu, b_glu, w_br_attn, w_br_ret, w_br_s5, w_out):
    f32 = jnp.float32
    n_dec, n_pages = page_table.shape
    past_len = n_pages * PAGE_SIZE
    Bp, Lp = x_prompt.shape[0], x_prompt.shape[1]
    Ls = x_sample.shape[1]
    pos_p = jnp.arange(Lp)
    pos_s = past_len + jnp.arange(Ls)
    yp, ys = x_prompt, x_sample
    kp_l, vp_l, rp_l, spr_l, spi_l = [], [], [], [], []
    ks_l, vs_l, rs_l, ssr_l, ssi_l = [], [], [], [], []
    for l in range(DEPTH):
        p = dict(norm_gain=norm_gain[l], w_in=w_in[l], q_norm_gain=q_norm_gain[l],
                 k_norm_gain=k_norm_gain[l], lam_q1=lam_q1[l], lam_k1=lam_k1[l], lam_q2=lam_q2[l],
                 lam_k2=lam_k2[l], attn_out_gain=attn_out_gain[l], ret_out_gain=ret_out_gain[l],
                 s5_lam_re=s5_lam_re[l], s5_lam_im=s5_lam_im[l], s5_log_dt=s5_log_dt[l],
                 s5_b_re=s5_b_re[l], s5_b_im=s5_b_im[l], s5_c_re=s5_c_re[l], s5_c_im=s5_c_im[l],
                 s5_d=s5_d[l], w_glu=w_glu[l], b_glu=b_glu[l], w_br_attn=w_br_attn[l],
                 w_br_ret=w_br_ret[l], w_br_s5=w_br_s5[l], w_out=w_out[l])
        lam_init = 0.8 - 0.6 * math.exp(-0.3 * l)
        yp, kp, vp, rp, spr, spi = _layer(
            yp, pos_p, None, None,
            jnp.zeros((Bp, R_HEADS, R_DK, R_DV), f32),
            jnp.zeros((Bp, S_GROUPS, S_STATE), f32), jnp.zeros((Bp, S_GROUPS, S_STATE), f32),
            p, lam_init)
        k_past = cache_k[l][page_table].reshape(n_dec, past_len, A_HEADS, 2, A_DH)
        v_past = cache_v[l][page_table].reshape(n_dec, past_len, A_HEADS, A_DV)
        ys, kn, vn, rn, snr, sni = _layer(ys, pos_s, k_past, v_past, state_ret[l],
                                          state_s5_re[l], state_s5_im[l], p, lam_init)
        kp_l.append(kp); vp_l.append(vp); rp_l.append(rp); spr_l.append(spr); spi_l.append(spi)
        ks_l.append(kn); vs_l.append(vn); rs_l.append(rn); ssr_l.append(snr); ssi_l.append(sni)
    return (yp, ys,
            jnp.stack(kp_l), jnp.stack(vp_l), jnp.stack(rp_l), jnp.stack(spr_l), jnp.stack(spi_l),
            jnp.stack(ks_l), jnp.stack(vs_l), jnp.stack(rs_l), jnp.stack(ssr_l), jnp.stack(ssi_l))
```

```python
import functools
import math

import jax
import jax.numpy as jnp
import numpy as np
from jax import lax
from jax.experimental import pallas as pl
from jax.experimental.pallas import tpu as pltpu

F32 = jnp.float32
BF16 = jnp.bfloat16

LANES = 128
SUBLANES = 8
VMEM_LIMIT = 56 * 1024 * 1024

D_MODEL = 1024
BRANCH_W = D_MODEL // 2
A_HEADS = 4
A_DH = BRANCH_W // (2 * A_HEADS)
A_DV = 2 * A_DH
R_HEADS = 4
R_DK = BRANCH_W // R_HEADS
R_DV = BRANCH_W // R_HEADS
S_GROUP = 16
S_GROUPS = BRANCH_W // S_GROUP
S_STATE = 64
S_CH = S_GROUPS * S_STATE
N_BRANCH = 3
N_MIX = 10 * BRANCH_W
PAGE_SIZE = 128
ROPE_THETA = 10000.0
EPS = 1e-6
RET_LOG_G = tuple(math.log1p(-2.0 ** (-5.0 - h)) for h in range(R_HEADS))

SEC_AQ, SEC_AK, SEC_AV, SEC_AG, SEC_RQ, SEC_RK, SEC_RV, SEC_RG, SEC_SU, SEC_SG = range(10)


def _cparams(sem):
    return pltpu.CompilerParams(dimension_semantics=sem, vmem_limit_bytes=VMEM_LIMIT)


def _silu(x):
    return x * jax.nn.sigmoid(x)


def _gelu_tanh(x):
    return 0.5 * x * (1.0 + jnp.tanh(math.sqrt(2.0 / math.pi) * (x + 0.044715 * x * x * x)))


def _split_dot(x, w):
    hi = x.astype(BF16)
    lo = (x - hi.astype(F32)).astype(BF16)
    return (jnp.dot(hi, w, preferred_element_type=F32) + jnp.dot(lo, w, preferred_element_type=F32))


def _inproj_kernel(x_ref, g_ref, w_ref, cosa_ref, sina_ref, cosr_ref, sinr_ref, qg_ref, kg_ref, grp_ref,
                   q_ref, kf_ref, vf_ref, kb_ref, vb_ref, sag_ref, rq_ref, rk_ref, rv_ref, srg_ref,
                   su_ref, ssg_ref, h_scr):
    j = pl.program_id(1)

    @pl.when(j == 0)
    def _():
        x = x_ref[...]
        ms = jnp.mean(x * x, axis=-1, keepdims=True)
        h_scr[...] = (x * lax.rsqrt(ms + EPS) * g_ref[...]).astype(BF16)

    z = jnp.dot(h_scr[...], w_ref[...], preferred_element_type=F32)
    lane = lax.broadcasted_iota(jnp.int32, (z.shape[0], LANES), 1)

    def head(t, h):
        return t[:, h * LANES:(h + 1) * LANES]

    def qk_norm_rot(zh, gain):
        ss = _split_dot(zh * zh, grp_ref[...])
        y = zh * lax.rsqrt(ss * (1.0 / A_DH) + EPS) * gain
        half = A_DH // 2
        swapped = jnp.where((lane & half) == 0, pltpu.roll(y, LANES - half, 1), pltpu.roll(y, half, 1))
        return y * cosa_ref[...] + swapped * sina_ref[...]

    def ret_rot(zh):
        return zh * cosr_ref[...] + pltpu.roll(zh, R_DK // 2, 1) * sinr_ref[...]

    @pl.when(j == SEC_AQ)
    def _():
        for h in range(A_HEADS):
            y = qk_norm_rot(head(z, h), qg_ref[...]) * (A_DH ** -0.5)
            q_ref[:, h * LANES:(h + 1) * LANES] = y.astype(BF16)

    @pl.when(j == SEC_AK)
    def _():
        for h in range(A_HEADS):
            y = qk_norm_rot(head(z, h), kg_ref[...])
            kf_ref[:, h * LANES:(h + 1) * LANES] = y
            kb_ref[:, h * LANES:(h + 1) * LANES] = y.astype(BF16)

    @pl.when(j == SEC_AV)
    def _():
        vf_ref[...] = z
        vb_ref[...] = z.astype(BF16)

    @pl.when(j == SEC_AG)
    def _():
        sag_ref[...] = _silu(z).astype(BF16)

    @pl.when(j == SEC_RQ)
    def _():
        for h in range(R_HEADS):
            rq_ref[:, h * LANES:(h + 1) * LANES] = ret_rot(head(z, h)).astype(rq_ref.dtype)

    @pl.when(j == SEC_RK)
    def _():
        for h in range(R_HEADS):
            rk_ref[:, h * LANES:(h + 1) * LANES] = (ret_rot(head(z, h)) * (R_DK ** -0.5)).astype(rk_ref.dtype)

    @pl.when(j == SEC_RV)
    def _():
        rv_ref[...] = z.astype(rv_ref.dtype)

    @pl.when(j == SEC_RG)
    def _():
        srg_ref[...] = _silu(z).astype(BF16)

    @pl.when(j == SEC_SU)
    def _():
        su_ref[...] = z.astype(BF16)

    @pl.when(j == SEC_SG)
    def _():
        ssg_ref[...] = _silu(z).astype(BF16)


def _inproj(x, p, tabs, tm, rows_per_seq, ret_dtype):
    n = x.shape[0]
    nblk = n // tm
    tab_blocks = rows_per_seq // tm if rows_per_seq >= tm else 1
    row = lambda i, j: (i, 0)
    tab = lambda i, j: (i % tab_blocks, 0)
    const = lambda i, j: (0, 0)
    sec = pl.BlockSpec((tm, BRANCH_W), row)
    tspec = pl.BlockSpec((tm, LANES), tab)
    names = ['q', 'kf', 'vf', 'kb', 'vb', 'sag', 'rq', 'rk', 'rv', 'srg', 'su', 'ssg']
    dts = [BF16, F32, F32, BF16, BF16, BF16, ret_dtype, ret_dtype, ret_dtype, BF16, BF16, BF16]
    outs = pl.pallas_call(
        _inproj_kernel,
        grid=(nblk, N_MIX // BRANCH_W),
        in_specs=[pl.BlockSpec((tm, D_MODEL), row),
                  pl.BlockSpec((1, D_MODEL), const),
                  pl.BlockSpec((D_MODEL, BRANCH_W), lambda i, j: (0, j)),
                  tspec, tspec, tspec, tspec,
                  pl.BlockSpec((1, LANES), const), pl.BlockSpec((1, LANES), const),
                  pl.BlockSpec((LANES, LANES), const)],
        out_specs=[sec] * len(names),
        out_shape=[jax.ShapeDtypeStruct((n, BRANCH_W), dt) for dt in dts],
        scratch_shapes=[pltpu.VMEM((tm, D_MODEL), BF16)],
        compiler_params=_cparams(("arbitrary", "arbitrary")),
        name="inproj",
    )(x, p['norm_gain'], p['w_mix'], tabs['cosa'], tabs['sina'], tabs['cosr'], tabs['sinr'],
      p['q_gain'], p['k_gain'], p['grp'])
    return dict(zip(names, outs))


def _diff_lambda(lq1, lk1, lq2, lk2, lam_init):
    return (jnp.exp(jnp.sum(lq1 * lk1, axis=-1, keepdims=True))
            - jnp.exp(jnp.sum(lq2 * lk2, axis=-1, keepdims=True)) + lam_init)


def _attn_finish(o, gain, sag, lam_init):
    ms = jnp.mean(o * o, axis=-1, keepdims=True)
    return o * lax.rsqrt(ms + EPS) * gain * (1.0 - lam_init) * sag


def _attn_prompt_kernel(qi_ref, ki_ref, q_ref, k_ref, v_ref, sag_ref, lq1_ref, lk1_ref, lq2_ref, lk2_ref,
                        gain_ref, o_ref, qs_scr, m_scr, l_scr, acc_scr, *, lam_init, tq):
    t = pl.program_id(1)
    qi = qi_ref[t]
    ki = ki_ref[t]
    lane = lax.broadcasted_iota(jnp.int32, (tq, LANES), 1)

    @pl.when(ki == 0)
    def _():
        for h in range(A_HEADS):
            qh = q_ref[:, h * LANES:(h + 1) * LANES]
            zero = jnp.zeros_like(qh)
            qs_scr[h, 0:tq, :] = jnp.where(lane < A_DH, qh, zero)
            qs_scr[h, tq:2 * tq, :] = jnp.where(lane >= A_DH, qh, zero)
        m_scr[...] = jnp.full(m_scr.shape, -jnp.inf, F32)
        l_scr[...] = jnp.zeros(l_scr.shape, F32)
        acc_scr[...] = jnp.zeros(acc_scr.shape, F32)

    def update(masked):
        for h in range(A_HEADS):
            kh = k_ref[:, h * LANES:(h + 1) * LANES]
            vh = v_ref[:, h * LANES:(h + 1) * LANES]
            s = lax.dot_general(qs_scr[h], kh, (((1,), (1,)), ((), ())), preferred_element_type=F32)
            if masked:
                r = lax.broadcasted_iota(jnp.int32, s.shape, 0)
                c = lax.broadcasted_iota(jnp.int32, s.shape, 1)
                r = jnp.where(r >= tq, r - tq, r)
                s = jnp.where(c <= r, s, -jnp.inf)
            m_prev = m_scr[h]
            m_new = jnp.maximum(m_prev, jnp.max(s, axis=-1, keepdims=True))
            alpha = jnp.exp(m_prev - m_new)
            pexp = jnp.exp(s - m_new)
            l_scr[h] = alpha * l_scr[h] + jnp.sum(pexp, axis=-1, keepdims=True)
            acc_scr[h] = alpha * acc_scr[h] + jnp.dot(pexp.astype(BF16), vh, preferred_element_type=F32)
            m_scr[h] = m_new

    @pl.when(ki < qi)
    def _():
        update(False)

    @pl.when(ki == qi)
    def _():
        update(True)
        lam = _diff_lambda(lq1_ref[...], lk1_ref[...], lq2_ref[...], lk2_ref[...], lam_init)
        for h in range(A_HEADS):
            acc = acc_scr[h]
            l = l_scr[h]
            o = acc[0:tq] / l[0:tq] - lam * (acc[tq:2 * tq] / l[tq:2 * tq])
            sl = slice(h * LANES, (h + 1) * LANES)
            o_ref[:, sl] = _attn_finish(o, gain_ref[:, sl], sag_ref[:, sl].astype(F32), lam_init).astype(BF16)


def _attn_prompt(mix, p, nb, seq, lam_init, tq):
    n = nb * seq
    nq = seq // tq
    pairs = [(a, b) for a in range(nq) for b in range(a + 1)]
    qi = jnp.asarray(np.array([a for a, _ in pairs], np.int32))
    ki = jnp.asarray(np.array([b for _, b in pairs], np.int32))
    qmap = lambda b, t, qi, ki: (b * nq + qi[t], 0)
    kmap = lambda b, t, qi, ki: (b * nq + ki[t], 0)
    const = lambda b, t, qi, ki: (0, 0)
    blk = lambda m: pl.BlockSpec((tq, BRANCH_W), m)
    vec = pl.BlockSpec((1, A_DH), const)
    return pl.pallas_call(
        functools.partial(_attn_prompt_kernel, lam_init=lam_init, tq=tq),
        grid_spec=pltpu.PrefetchScalarGridSpec(
            num_scalar_prefetch=2,
            grid=(nb, len(pairs)),
            in_specs=[blk(qmap), blk(kmap), blk(kmap), blk(qmap), vec, vec, vec, vec,
                      pl.BlockSpec((1, BRANCH_W), const)],
            out_specs=blk(qmap),
            scratch_shapes=[pltpu.VMEM((A_HEADS, 2 * tq, LANES), BF16),
                            pltpu.VMEM((A_HEADS, 2 * tq, 1), F32),
                            pltpu.VMEM((A_HEADS, 2 * tq, 1), F32),
                            pltpu.VMEM((A_HEADS, 2 * tq, LANES), F32)]),
        out_shape=jax.ShapeDtypeStruct((n, BRANCH_W), BF16),
        compiler_params=_cparams(("arbitrary", "arbitrary")),
        name="attn_prompt",
    )(qi, ki, mix['q'], mix['kb'], mix['vb'], mix['sag'], p['lam_q1'], p['lam_k1'], p['lam_q2'], p['lam_k2'],
      p['attn_gain'])


def _attn_sample_kernel(pt_ref, *refs, lam_init, pages):
    kc_refs = refs[0:pages]
    vc_refs = refs[pages:2 * pages]
    (q_ref, kn_ref, vn_ref, sag_ref, lq1_ref, lk1_ref, lq2_ref, lk2_ref, gain_ref,
     o_ref, m_scr, l_scr, acc_scr) = refs[2 * pages:]
    s_id = pl.program_id(1)
    n_maps = 2 * A_HEADS
    row = lax.broadcasted_iota(jnp.int32, (n_maps, BRANCH_W), 0)
    lane = lax.broadcasted_iota(jnp.int32, (n_maps, BRANCH_W), 1)
    qt = jnp.where(lane // A_DH == row, jnp.broadcast_to(q_ref[0], (n_maps, BRANCH_W)), 0.0)

    @pl.when(s_id == 0)
    def _():
        m_scr[...] = jnp.full(m_scr.shape, -jnp.inf, F32)
        l_scr[...] = jnp.zeros(l_scr.shape, F32)
        acc_scr[...] = jnp.zeros(acc_scr.shape, F32)

    kcat = jnp.concatenate([r[...].astype(BF16) for r in kc_refs], axis=0)
    vcat = jnp.concatenate([r[...].astype(BF16) for r in vc_refs], axis=0)
    s = lax.dot_general(qt.astype(BF16), kcat, (((1,), (1,)), ((), ())), preferred_element_type=F32)
    m_prev = m_scr[...]
    m_new = jnp.maximum(m_prev, jnp.max(s, axis=-1, keepdims=True))
    alpha = jnp.exp(m_prev - m_new)
    pexp = jnp.exp(s - m_new)
    l_scr[...] = alpha * l_scr[...] + jnp.sum(pexp, axis=-1, keepdims=True)
    acc_scr[...] = alpha * acc_scr[...] + jnp.dot(pexp.astype(BF16), vcat, preferred_element_type=F32)
    m_scr[...] = m_new

    @pl.when(s_id == pl.num_programs(1) - 1)
    def _():
        s_new = jnp.sum(qt * kn_ref[0], axis=-1, keepdims=True)
        m_prev = m_scr[...]
        m_fin = jnp.maximum(m_prev, s_new)
        alpha = jnp.exp(m_prev - m_fin)
        p_new = jnp.exp(s_new - m_fin)
        l = alpha * l_scr[...] + p_new
        acc = alpha * acc_scr[...] + p_new * vn_ref[0]
        lam = _diff_lambda(lq1_ref[...], lk1_ref[...], lq2_ref[...], lk2_ref[...], lam_init)
        coef = jnp.where(row % 2 == 0, 1.0, -lam) / l
        coef = jnp.where(lane // LANES == row // 2, coef, 0.0)
        o = jnp.sum(acc * coef, axis=0, keepdims=True)
        for h in range(A_HEADS):
            sl = slice(h * LANES, (h + 1) * LANES)
            o_ref[0, :, sl] = _attn_finish(o[:, sl], gain_ref[:, sl], sag_ref[0, :, sl].astype(F32),
                                           lam_init).astype(BF16)


def _attn_sample(mix, p, cache_k, cache_v, page_table, layer, lam_init, pages):
    nd, n_pages = page_table.shape
    steps = n_pages // pages
    pool = cache_k.shape[1]
    ck = cache_k.reshape(cache_k.shape[0], pool, PAGE_SIZE, BRANCH_W)
    cv = cache_v.reshape(cache_v.shape[0], pool, PAGE_SIZE, BRANCH_W)

    def page_spec(i):
        return pl.BlockSpec((None, None, PAGE_SIZE, BRANCH_W),
                            lambda b, s, pt: (layer, pt[b * n_pages + s * pages + i], 0, 0))

    tok = pl.BlockSpec((1, 1, BRANCH_W), lambda b, s, pt: (b, 0, 0))
    const = lambda b, s, pt: (0, 0)
    vec = pl.BlockSpec((1, A_DH), const)
    r3 = lambda a: a.reshape(nd, 1, BRANCH_W)
    n_maps = 2 * A_HEADS
    out = pl.pallas_call(
        functools.partial(_attn_sample_kernel, lam_init=lam_init, pages=pages),
        grid_spec=pltpu.PrefetchScalarGridSpec(
            num_scalar_prefetch=1,
            grid=(nd, steps),
            in_specs=[page_spec(i) for i in range(pages)] * 2
                     + [tok, tok, tok, tok, vec, vec, vec, vec, pl.BlockSpec((1, BRANCH_W), const)],
            out_specs=tok,
            scratch_shapes=[pltpu.VMEM((n_maps, 1), F32), pltpu.VMEM((n_maps, 1), F32),
                            pltpu.VMEM((n_maps, BRANCH_W), F32)]),
        out_shape=jax.ShapeDtypeStruct((nd, 1, BRANCH_W), BF16),
        compiler_params=_cparams(("arbitrary", "arbitrary")),
        name="attn_sample",
    )(page_table.reshape(-1), *([ck] * pages), *([cv] * pages),
      r3(mix['q'].astype(F32)), r3(mix['kf']), r3(mix['vf']), r3(mix['sag']),
      p['lam_q1'], p['lam_k1'], p['lam_q2'], p['lam_k2'], p['attn_gain'])
    return out.reshape(nd, BRANCH_W)


def _ret_finish(o, gain, srg):
    ms = jnp.mean(o * o, axis=-1, keepdims=True)
    return o * lax.rsqrt(ms + EPS) * gain * srg


def _ret_prompt_kernel(q_ref, k_ref, v_ref, srg_ref, gain_ref, o_ref, st_ref, s_scr, dm_scr, qd_scr, kd_scr, *, c):
    ci = pl.program_id(1)

    @pl.when(ci == 0)
    def _():
        s_scr[...] = jnp.zeros(s_scr.shape, F32)
        i = lax.broadcasted_iota(jnp.int32, (c, c), 0)
        jj = lax.broadcasted_iota(jnp.int32, (c, c), 1)
        d = (i - jj).astype(F32)
        r = lax.broadcasted_iota(jnp.int32, (c, LANES), 0).astype(F32)
        for h in range(R_HEADS):
            lg = RET_LOG_G[h]
            dm_scr[h] = jnp.where(i >= jj, jnp.exp(d * lg), 0.0)
            qd_scr[h] = jnp.exp((r + 1.0) * lg)
            kd_scr[h] = jnp.exp((c - 1.0 - r) * lg)

    for h in range(R_HEADS):
        sl = slice(h * LANES, (h + 1) * LANES)
        q = q_ref[:, sl]
        k = k_ref[:, sl]
        v = v_ref[:, sl]
        att = lax.dot_general(q, k, (((1,), (1,)), ((), ())), preferred_element_type=F32) * dm_scr[h]
        s_old = s_scr[h]
        o = (jnp.dot(att.astype(BF16), v, preferred_element_type=F32)
             + jnp.dot((q.astype(F32) * qd_scr[h]).astype(BF16), s_old.astype(BF16), preferred_element_type=F32))
        kd = (k.astype(F32) * kd_scr[h]).astype(BF16)
        s_new = math.exp(c * RET_LOG_G[h]) * s_old + lax.dot_general(
            kd, v, (((0,), (0,)), ((), ())), preferred_element_type=F32)
        s_scr[h] = s_new
        o_ref[:, sl] = _ret_finish(o, gain_ref[:, sl], srg_ref[:, sl].astype(F32)).astype(BF16)

    @pl.when(ci == pl.num_programs(1) - 1)
    def _():
        st_ref[0] = s_scr[...]


def _ret_prompt(mix, p, nb, seq, c):
    n = nb * seq
    nc = seq // c
    row = lambda b, i: (b * nc + i, 0)
    blk = pl.BlockSpec((c, BRANCH_W), row)
    return pl.pallas_call(
        functools.partial(_ret_prompt_kernel, c=c),
        grid=(nb, nc),
        in_specs=[blk, blk, blk, blk, pl.BlockSpec((1, BRANCH_W), lambda b, i: (0, 0))],
        out_specs=[blk, pl.BlockSpec((1, R_HEADS, R_DK, R_DV), lambda b, i: (b, 0, 0, 0))],
        out_shape=[jax.ShapeDtypeStruct((n, BRANCH_W), BF16),
                   jax.ShapeDtypeStruct((nb, R_HEADS, R_DK, R_DV), F32)],
        scratch_shapes=[pltpu.VMEM((R_HEADS, R_DK, R_DV), F32), pltpu.VMEM((R_HEADS, c, c), F32),
                        pltpu.VMEM((R_HEADS, c, LANES), F32), pltpu.VMEM((R_HEADS, c, LANES), F32)],
        compiler_params=_cparams(("arbitrary", "arbitrary")),
        name="ret_prompt",
    )(mix['rq'], mix['rk'], mix['rv'], mix['srg'], p['ret_gain'])


def _ret_sample_kernel(q_ref, k_ref, v_ref, srg_ref, gain_ref, s0_ref, o_ref, st_ref, *, nb):
    for b in range(nb):
        for h in range(R_HEADS):
            sl = slice(h * LANES, (h + 1) * LANES)
            g = math.exp(RET_LOG_G[h])
            q = q_ref[b, :, sl]
            k = k_ref[b, :, sl]
            v = v_ref[b, :, sl]
            s0 = s0_ref[b, h]
            qcol = jnp.transpose(jnp.broadcast_to(q, (SUBLANES, LANES)))[:, 0:1]
            kcol = jnp.transpose(jnp.broadcast_to(k, (SUBLANES, LANES)))[:, 0:1]
            o = jnp.sum(q * k, axis=-1, keepdims=True) * v + g * jnp.sum(qcol * s0, axis=0, keepdims=True)
            st_ref[b, h] = g * s0 + kcol * v
            o_ref[b, :, sl] = _ret_finish(o, gain_ref[:, sl], srg_ref[b, :, sl].astype(F32)).astype(BF16)


def _ret_sample(mix, p, s0, nb_blk):
    nd = s0.shape[0]
    r3 = lambda a: a.reshape(nd, 1, BRANCH_W)
    tok = pl.BlockSpec((nb_blk, 1, BRANCH_W), lambda i: (i, 0, 0))
    st = pl.BlockSpec((nb_blk, R_HEADS, R_DK, R_DV), lambda i: (i, 0, 0, 0))
    o, s1 = pl.pallas_call(
        functools.partial(_ret_sample_kernel, nb=nb_blk),
        grid=(nd // nb_blk,),
        in_specs=[tok, tok, tok, tok, pl.BlockSpec((1, BRANCH_W), lambda i: (0, 0)), st],
        out_specs=[tok, st],
        out_shape=[jax.ShapeDtypeStruct((nd, 1, BRANCH_W), BF16),
                   jax.ShapeDtypeStruct((nd, R_HEADS, R_DK, R_DV), F32)],
        compiler_params=_cparams(("arbitrary",)),
        name="ret_sample",
    )(r3(mix['rq']), r3(mix['rk']), r3(mix['rv']), r3(mix['srg']), p['ret_gain'], s0)
    return o.reshape(nd, BRANCH_W), s1


S5_LANE_BLK = 512


def _cmul_add(ar, ai, hr, hi, br, bi):
    return ar * hr - ai * hi + br, ar * hi + ai * hr + bi


def _s5_prompt_kernel(u_ref, bmat_ref, cmat_ref, d_ref, ar_ref, ai_ref, atr_ref, ati_ref,
                      y_ref, hre_ref, him_ref, up_scr, bu_scr, yp_scr, car_scr, *, tm):
    it = pl.program_id(1)
    tp = tm // SUBLANES

    @pl.when(it == 0)
    def _():
        car_scr[...] = jnp.zeros(car_scr.shape, F32)

    u = u_ref[...].astype(F32)
    nlb = BRANCH_W // LANES
    for j in range(SUBLANES):
        for kb in range(nlb):
            up_scr[kb, pl.ds(j, tp, stride=SUBLANES), :] = u[j * tp:(j + 1) * tp, kb * LANES:(kb + 1) * LANES]
    up = jnp.concatenate([up_scr[kb] for kb in range(nlb)], axis=1)
    bu_scr[...] = jnp.dot(up.astype(BF16), bmat_ref[...], preferred_element_type=F32)

    for cb in range(S_CH // S5_LANE_BLK):
        re = slice(cb * S5_LANE_BLK, (cb + 1) * S5_LANE_BLK)
        im = slice(S_CH + cb * S5_LANE_BLK, S_CH + (cb + 1) * S5_LANE_BLK)
        ar = jnp.broadcast_to(ar_ref[:, re], (SUBLANES, S5_LANE_BLK))
        ai = jnp.broadcast_to(ai_ref[:, re], (SUBLANES, S5_LANE_BLK))

        def local_step(i, carry):
            hr, hi = carry
            r0 = pl.multiple_of(i * SUBLANES, SUBLANES)
            return _cmul_add(ar, ai, hr, hi, bu_scr[pl.ds(r0, SUBLANES), re], bu_scr[pl.ds(r0, SUBLANES), im])

        zero = jnp.zeros((SUBLANES, S5_LANE_BLK), F32)
        er, ei = lax.fori_loop(0, tp, local_step, (zero, zero))

        atr = atr_ref[:, re]
        ati = ati_ref[:, re]
        cr = car_scr[0:1, re]
        ci = car_scr[0:1, im]
        rows_r, rows_i = [cr], [ci]
        for j in range(1, SUBLANES):
            cr, ci = _cmul_add(atr, ati, cr, ci, er[j - 1:j], ei[j - 1:j])
            rows_r.append(cr)
            rows_i.append(ci)
        cr, ci = _cmul_add(atr, ati, cr, ci, er[SUBLANES - 1:SUBLANES], ei[SUBLANES - 1:SUBLANES])
        car_scr[0:1, re] = cr
        car_scr[0:1, im] = ci
        sr = jnp.concatenate(rows_r, axis=0)
        si = jnp.concatenate(rows_i, axis=0)

        def full_step(i, carry):
            hr, hi = carry
            r0 = pl.multiple_of(i * SUBLANES, SUBLANES)
            hr, hi = _cmul_add(ar, ai, hr, hi, bu_scr[pl.ds(r0, SUBLANES), re], bu_scr[pl.ds(r0, SUBLANES), im])
            bu_scr[pl.ds(r0, SUBLANES), re] = hr
            bu_scr[pl.ds(r0, SUBLANES), im] = hi
            return hr, hi

        lax.fori_loop(0, tp, full_step, (sr, si))

    yp = jnp.dot(bu_scr[...].astype(BF16), cmat_ref[...], preferred_element_type=F32) + d_ref[...] * up
    for kb in range(nlb):
        yp_scr[kb] = yp[:, kb * LANES:(kb + 1) * LANES]
    for j in range(SUBLANES):
        for kb in range(nlb):
            y_ref[j * tp:(j + 1) * tp, kb * LANES:(kb + 1) * LANES] = _gelu_tanh(
                yp_scr[kb, pl.ds(j, tp, stride=SUBLANES), :]).astype(BF16)

    @pl.when(it == pl.num_programs(1) - 1)
    def _():
        hre_ref[0] = car_scr[0:1, 0:S_CH]
        him_ref[0] = car_scr[0:1, S_CH:2 * S_CH]


def _s5_prompt(mix, s5, nb, seq, tm):
    n = nb * seq
    nt = seq // tm
    row = lambda b, i: (b * nt + i, 0)
    const = lambda b, i: (0, 0)
    blk = pl.BlockSpec((tm, BRANCH_W), row)
    vec = pl.BlockSpec((1, S_CH), const)
    st = pl.BlockSpec((1, 1, S_CH), lambda b, i: (b, 0, 0))
    y, hre, him = pl.pallas_call(
        functools.partial(_s5_prompt_kernel, tm=tm),
        grid=(nb, nt),
        in_specs=[blk, pl.BlockSpec((BRANCH_W, 2 * S_CH), const), pl.BlockSpec((2 * S_CH, BRANCH_W), const),
                  pl.BlockSpec((1, BRANCH_W), const), vec, vec, vec, vec],
        out_specs=[blk, st, st],
        out_shape=[jax.ShapeDtypeStruct((n, BRANCH_W), BF16),
                   jax.ShapeDtypeStruct((nb, 1, S_CH), F32), jax.ShapeDtypeStruct((nb, 1, S_CH), F32)],
        scratch_shapes=[pltpu.VMEM((BRANCH_W // LANES, tm, LANES), F32), pltpu.VMEM((tm, 2 * S_CH), F32),
                        pltpu.VMEM((BRANCH_W // LANES, tm, LANES), F32), pltpu.VMEM((SUBLANES, 2 * S_CH), F32)],
        compiler_params=_cparams(("arbitrary", "arbitrary")),
        name="s5_prompt",
    )(mix['su'], s5['bmat'], s5['cmat'], s5['d'], s5['a_re'], s5['a_im'], s5['at_re'][tm // SUBLANES],
      s5['at_im'][tm // SUBLANES])
    return y, hre.reshape(nb, S_GROUPS, S_STATE), him.reshape(nb, S_GROUPS, S_STATE)


def _s5_sample_kernel(u_ref, bmat_ref, cmat_ref, d_ref, ar_ref, ai_ref, h0r_ref, h0i_ref, y_ref, hr_ref, hi_ref):
    u = u_ref[...]
    bu = jnp.dot(u, bmat_ref[...], preferred_element_type=F32)
    hr, hi = _cmul_add(ar_ref[...], ai_ref[...], h0r_ref[...], h0i_ref[...], bu[:, 0:S_CH], bu[:, S_CH:2 * S_CH])
    hr_ref[...] = hr
    hi_ref[...] = hi
    h = jnp.concatenate([hr, hi], axis=-1).astype(BF16)
    y = jnp.dot(h, cmat_ref[...], preferred_element_type=F32) + d_ref[...] * u.astype(F32)
    y_ref[...] = _gelu_tanh(y).astype(BF16)


def _s5_sample(mix, s5, h0_re, h0_im):
    nd = h0_re.shape[0]
    y, hr, hi = pl.pallas_call(
        _s5_sample_kernel,
        out_shape=[jax.ShapeDtypeStruct((nd, BRANCH_W), BF16),
                   jax.ShapeDtypeStruct((nd, S_CH), F32), jax.ShapeDtypeStruct((nd, S_CH), F32)],
        compiler_params=pltpu.CompilerParams(vmem_limit_bytes=VMEM_LIMIT),
        name="s5_sample",
    )(mix['su'], s5['bmat'], s5['cmat'], s5['d'], s5['a_re'], s5['a_im'],
      h0_re.reshape(nd, S_CH), h0_im.reshape(nd, S_CH))
    return y, hr.reshape(nd, S_GROUPS, S_STATE), hi.reshape(nd, S_GROUPS, S_STATE)


def _merge_kernel(x_ref, g_ref, wmg_ref, oa_ref, or_ref, ys_ref, ssg_ref, wba_ref, wbr_ref, wbs_ref,
                  wglu_ref, bglu_ref, wout_ref, out_ref):
    x = x_ref[...]
    ms = jnp.mean(x * x, axis=-1, keepdims=True)
    h = (x * lax.rsqrt(ms + EPS) * g_ref[...]).astype(BF16)
    gates = jax.nn.sigmoid(jnp.dot(h, wmg_ref[...], preferred_element_type=F32))
    ya = jnp.dot(oa_ref[...], wba_ref[...], preferred_element_type=F32)
    yr = jnp.dot(or_ref[...], wbr_ref[...], preferred_element_type=F32)
    glu = jnp.dot(ys_ref[...], wglu_ref[...], preferred_element_type=F32) + bglu_ref[...]
    s5 = glu[:, 0:BRANCH_W] * jax.nn.sigmoid(glu[:, BRANCH_W:2 * BRANCH_W]) * ssg_ref[...].astype(F32)
    y5 = jnp.dot(s5.astype(BF16), wbs_ref[...], preferred_element_type=F32)
    merged = (gates[:, 0:D_MODEL] * ya + gates[:, D_MODEL:2 * D_MODEL] * yr
              + gates[:, 2 * D_MODEL:3 * D_MODEL] * y5)
    out_ref[...] = x + jnp.dot(merged.astype(BF16), wout_ref[...], preferred_element_type=F32)


def _merge(x, o_attn, o_ret, y_s5, ssg, p, tm):
    n = x.shape[0]
    row = lambda i: (i, 0)
    const = lambda i: (0, 0)
    half = pl.BlockSpec((tm, BRANCH_W), row)
    full = lambda a: pl.BlockSpec(a.shape, const)
    ws = [p['w_br_attn'], p['w_br_ret'], p['w_br_s5'], p['w_glu'], p['b_glu'], p['w_out']]
    return pl.pallas_call(
        _merge_kernel,
        grid=(n // tm,),
        in_specs=[pl.BlockSpec((tm, D_MODEL), row), full(p['norm_gain']), full(p['w_mg']),
                  half, half, half, half] + [full(w) for w in ws],
        out_specs=pl.BlockSpec((tm, D_MODEL), row),
        out_shape=jax.ShapeDtypeStruct((n, D_MODEL), F32),
        compiler_params=_cparams(("arbitrary",)),
        name="merge",
    )(x, p['norm_gain'], p['w_mg'], o_attn, o_ret, y_s5, ssg, *ws)


def _rot_tables(pos, rows):
    pos = pos.astype(F32)[:, None]

    def tab(half, reps):
        inv = ROPE_THETA ** (-jnp.arange(half, dtype=F32) / half)
        ang = pos * inv[None, :]
        cos, sin = jnp.cos(ang), jnp.sin(ang)
        c = jnp.tile(jnp.concatenate([cos, cos], axis=-1), (1, reps))
        s = jnp.tile(jnp.concatenate([-sin, sin], axis=-1), (1, reps))
        if c.shape[0] != rows:
            c, s = jnp.broadcast_to(c, (rows, LANES)), jnp.broadcast_to(s, (rows, LANES))
        return c, s

    cosa, sina = tab(A_DH // 2, LANES // A_DH)
    cosr, sinr = tab(R_DK // 2, 1)
    return dict(cosa=cosa, sina=sina, cosr=cosr, sinr=sinr)


def _s5_params(lam_re, lam_im, log_dt, b_re, b_im, c_re, c_im, d_skip, powers):
    dt = jnp.exp(log_dt.astype(F32))[:, None]
    lre, lim = lam_re.astype(F32), lam_im.astype(F32)
    mag = jnp.exp(lre * dt)
    ab_re, ab_im = mag * jnp.cos(lim * dt), mag * jnp.sin(lim * dt)
    den = lre * lre + lim * lim
    nre = ab_re - 1.0
    cre = (nre * lre + ab_im * lim) / den
    cim = (ab_im * lre - nre * lim) / den
    b_re, b_im = b_re.astype(F32), b_im.astype(F32)
    bb_re = cre[..., None] * b_re - cim[..., None] * b_im
    bb_im = cre[..., None] * b_im + cim[..., None] * b_re
    eye = jnp.eye(S_GROUPS, dtype=F32)

    def in_mat(bb):
        return jnp.einsum('gpc,gh->gchp', bb, eye).reshape(BRANCH_W, S_CH)

    def out_mat(cc):
        return jnp.einsum('gcp,gh->gphc', cc.astype(F32), eye).reshape(S_CH, BRANCH_W)

    bmat = jnp.concatenate([in_mat(bb_re), in_mat(bb_im)], axis=1).astype(BF16)
    cmat = jnp.concatenate([out_mat(c_re), -out_mat(c_im)], axis=0).astype(BF16)
    a_re, a_im = ab_re.reshape(1, S_CH), ab_im.reshape(1, S_CH)
    at_re, at_im = {1: a_re}, {1: a_im}
    pr, pi, k = a_re, a_im, 1
    while k < max(powers):
        pr, pi, k = pr * pr - pi * pi, 2.0 * pr * pi, 2 * k
        at_re[k], at_im[k] = pr, pi
    return dict(bmat=bmat, cmat=cmat, d=d_skip.astype(F32).reshape(1, BRANCH_W), a_re=a_re, a_im=a_im,
                at_re=at_re, at_im=at_im)


TM_PROJ = 512
TQ_ATTN = 512
RET_CHUNK = 256
TM_S5 = 512
TM_MERGE = 512
SAMPLE_PAGES = 8
RET_SAMPLE_BLK = 8


def kernel(x_prompt, x_sample, cache_k, cache_v, state_ret, state_s5_re, state_s5_im, page_table, norm_gain, w_in, q_norm_gain, k_norm_gain, lam_q1, lam_k1, lam_q2, lam_k2, attn_out_gain, ret_out_gain, s5_lam_re, s5_lam_im, s5_log_dt, s5_b_re, s5_b_im, s5_c_re, s5_c_im, s5_d, w_glu, b_glu, w_br_attn, w_br_ret, w_br_s5, w_out):
    depth = w_in.shape[0]
    nb, seq, _ = x_prompt.shape
    nd, dseq, _ = x_sample.shape
    assert dseq == 1, "the sample group is a single-token step"
    n_pages = page_table.shape[1]
    past_len = n_pages * PAGE_SIZE
    tm_proj, tq, c_ret = min(TM_PROJ, seq), min(TQ_ATTN, seq), min(RET_CHUNK, seq)
    tm_s5, tm_merge = min(TM_S5, seq), min(TM_MERGE, seq)
    pages = min(SAMPLE_PAGES, n_pages)
    assert seq % tm_proj == 0 and seq % tq == 0 and seq % c_ret == 0 and seq % tm_s5 == 0 and seq % tm_merge == 0
    assert n_pages % pages == 0 and nd % RET_SAMPLE_BLK == 0

    tabs_p = _rot_tables(jnp.arange(seq), seq)
    tabs_s = _rot_tables(past_len + jnp.arange(1), nd)
    grp = (jnp.arange(LANES)[:, None] // A_DH == jnp.arange(LANES)[None, :] // A_DH).astype(BF16)

    yp = x_prompt.reshape(nb * seq, D_MODEL)
    ys = x_sample.reshape(nd, D_MODEL)
    outs = {k: [] for k in ('kp', 'vp', 'rp', 'spr', 'spi', 'ks', 'vs', 'rs', 'ssr', 'ssi')}
    for l in range(depth):
        lam_init = 0.8 - 0.6 * math.exp(-0.3 * l)
        w_l = w_in[l].astype(BF16)
        p = dict(
            norm_gain=norm_gain[l].reshape(1, D_MODEL), w_mix=w_l[:, :N_MIX], w_mg=w_l[:, N_MIX:], grp=grp,
            q_gain=jnp.tile(q_norm_gain[l], LANES // A_DH).reshape(1, LANES),
            k_gain=jnp.tile(k_norm_gain[l], LANES // A_DH).reshape(1, LANES),
            lam_q1=lam_q1[l].reshape(1, A_DH), lam_k1=lam_k1[l].reshape(1, A_DH),
            lam_q2=lam_q2[l].reshape(1, A_DH), lam_k2=lam_k2[l].reshape(1, A_DH),
            attn_gain=attn_out_gain[l].reshape(1, BRANCH_W), ret_gain=ret_out_gain[l].reshape(1, BRANCH_W),
            w_glu=w_glu[l].astype(BF16), b_glu=b_glu[l].reshape(1, 2 * BRANCH_W),
            w_br_attn=w_br_attn[l].astype(BF16), w_br_ret=w_br_ret[l].astype(BF16),
            w_br_s5=w_br_s5[l].astype(BF16), w_out=w_out[l].astype(BF16))
        s5 = _s5_params(s5_lam_re[l], s5_lam_im[l], s5_log_dt[l], s5_b_re[l], s5_b_im[l], s5_c_re[l], s5_c_im[l],
                        s5_d[l], (tm_s5 // SUBLANES,))

        mix = _inproj(yp, p, tabs_p, tm_proj, seq, BF16)
        o_attn = _attn_prompt(mix, p, nb, seq, lam_init, tq)
        o_ret, ret_state = _ret_prompt(mix, p, nb, seq, c_ret)
        y_s5, s_re, s_im = _s5_prompt(mix, s5, nb, seq, tm_s5)
        yp = _merge(yp, o_attn, o_ret, y_s5, mix['ssg'], p, tm_merge)
        outs['kp'].append(mix['kf'].reshape(nb, seq, A_HEADS, A_DV))
        outs['vp'].append(mix['vf'].reshape(nb, seq, A_HEADS, A_DV))
        outs['rp'].append(ret_state)
        outs['spr'].append(s_re)
        outs['spi'].append(s_im)

        mix = _inproj(ys, p, tabs_s, nd, nd, F32)
        o_attn = _attn_sample(mix, p, cache_k, cache_v, page_table, l, lam_init, pages)
        o_ret, ret_state = _ret_sample(mix, p, state_ret[l], RET_SAMPLE_BLK)
        y_s5, s_re, s_im = _s5_sample(mix, s5, state_s5_re[l], state_s5_im[l])
        ys = _merge(ys, o_attn, o_ret, y_s5, mix['ssg'], p, nd)
        outs['ks'].append(mix['kf'].reshape(nd, 1, A_HEADS, A_DV))
        outs['vs'].append(mix['vf'].reshape(nd, 1, A_HEADS, A_DV))
        outs['rs'].append(ret_state)
        outs['ssr'].append(s_re)
        outs['ssi'].append(s_im)

    st = lambda k: jnp.stack(outs[k])
    return (yp.reshape(nb, seq, D_MODEL), ys.reshape(nd, 1, D_MODEL),
            st('kp'), st('vp'), st('rp'), st('spr'), st('spi'),
            st('ks'), st('vs'), st('rs'), st('ssr'), st('ssi'))
```

```python
import functools
import math

import jax
import jax.numpy as jnp
import numpy as np
from jax import lax
from jax.experimental import pallas as pl
from jax.experimental.pallas import tpu as pltpu

F32 = jnp.float32
BF16 = jnp.bfloat16

LANES = 128
SUBLANES = 8
VMEM_LIMIT = 56 * 1024 * 1024

D_MODEL = 1024
BRANCH_W = D_MODEL // 2
A_HEADS = 4
A_DH = BRANCH_W // (2 * A_HEADS)
A_DV = 2 * A_DH
R_HEADS = 4
R_DK = BRANCH_W // R_HEADS
R_DV = BRANCH_W // R_HEADS
S_GROUP = 16
S_GROUPS = BRANCH_W // S_GROUP
S_STATE = 64
S_CH = S_GROUPS * S_STATE
N_BRANCH = 3
N_MIX = 10 * BRANCH_W
PAGE_SIZE = 128
ROPE_THETA = 10000.0
EPS = 1e-6
LOG2E = math.log2(math.e)
RET_LOG_G = tuple(math.log1p(-2.0 ** (-5.0 - h)) for h in range(R_HEADS))

SEC_AQ, SEC_AK, SEC_AV, SEC_AG, SEC_RQ, SEC_RK, SEC_RV, SEC_RG, SEC_SU, SEC_SG = range(10)


def _cparams(sem):
    return pltpu.CompilerParams(dimension_semantics=sem, vmem_limit_bytes=VMEM_LIMIT)


def _silu(x):
    return x * jax.nn.sigmoid(x)


def _gelu_tanh(x):
    return 0.5 * x * (1.0 + jnp.tanh(math.sqrt(2.0 / math.pi) * (x + 0.044715 * x * x * x)))


def _split_dot(x, w):
    hi = x.astype(BF16)
    lo = (x - hi.astype(F32)).astype(BF16)
    return (jnp.dot(hi, w, preferred_element_type=F32) + jnp.dot(lo, w, preferred_element_type=F32))


def _inproj_kernel(x_ref, g_ref, w_ref, cosa_ref, sina_ref, cosr_ref, sinr_ref, qg_ref, kg_ref, grp_ref,
                   q_ref, kf_ref, vf_ref, kb_ref, vb_ref, sag_ref, rq_ref, rk_ref, rv_ref, srg_ref,
                   su_ref, ssg_ref, h_scr):
    j = pl.program_id(1)

    @pl.when(j == 0)
    def _():
        x = x_ref[...]
        ms = jnp.mean(x * x, axis=-1, keepdims=True)
        h_scr[...] = (x * lax.rsqrt(ms + EPS) * g_ref[...]).astype(BF16)

    z = jnp.dot(h_scr[...], w_ref[...], preferred_element_type=F32)
    lane = lax.broadcasted_iota(jnp.int32, (z.shape[0], LANES), 1)

    def head(t, h):
        return t[:, h * LANES:(h + 1) * LANES]

    def qk_norm_rot(zh, gain):
        ss = _split_dot(zh * zh, grp_ref[...])
        y = zh * lax.rsqrt(ss * (1.0 / A_DH) + EPS) * gain
        half = A_DH // 2
        swapped = jnp.where((lane & half) == 0, pltpu.roll(y, LANES - half, 1), pltpu.roll(y, half, 1))
        return y * cosa_ref[...] + swapped * sina_ref[...]

    def ret_rot(zh):
        return zh * cosr_ref[...] + pltpu.roll(zh, R_DK // 2, 1) * sinr_ref[...]

    @pl.when(j == SEC_AQ)
    def _():
        for h in range(A_HEADS):
            y = qk_norm_rot(head(z, h), qg_ref[...]) * (A_DH ** -0.5 * LOG2E)
            q_ref[:, h * LANES:(h + 1) * LANES] = y.astype(BF16)

    @pl.when(j == SEC_AK)
    def _():
        for h in range(A_HEADS):
            y = qk_norm_rot(head(z, h), kg_ref[...])
            kf_ref[:, h * LANES:(h + 1) * LANES] = y
            kb_ref[:, h * LANES:(h + 1) * LANES] = y.astype(BF16)

    @pl.when(j == SEC_AV)
    def _():
        vf_ref[...] = z
        vb_ref[...] = z.astype(BF16)

    @pl.when(j == SEC_AG)
    def _():
        sag_ref[...] = _silu(z).astype(BF16)

    @pl.when(j == SEC_RQ)
    def _():
        for h in range(R_HEADS):
            rq_ref[:, h * LANES:(h + 1) * LANES] = ret_rot(head(z, h)).astype(rq_ref.dtype)

    @pl.when(j == SEC_RK)
    def _():
        for h in range(R_HEADS):
            rk_ref[:, h * LANES:(h + 1) * LANES] = (ret_rot(head(z, h)) * (R_DK ** -0.5)).astype(rk_ref.dtype)

    @pl.when(j == SEC_RV)
    def _():
        rv_ref[...] = z.astype(rv_ref.dtype)

    @pl.when(j == SEC_RG)
    def _():
        srg_ref[...] = _silu(z).astype(BF16)

    @pl.when(j == SEC_SU)
    def _():
        su_ref[...] = z.astype(BF16)

    @pl.when(j == SEC_SG)
    def _():
        ssg_ref[...] = _silu(z).astype(BF16)


def _inproj(x, p, tabs, tm, rows_per_seq, ret_dtype):
    n = x.shape[0]
    nblk = n // tm
    tab_blocks = rows_per_seq // tm if rows_per_seq >= tm else 1
    row = lambda i, j: (i, 0)
    tab = lambda i, j: (i % tab_blocks, 0)
    const = lambda i, j: (0, 0)
    sec = pl.BlockSpec((tm, BRANCH_W), row)
    tspec = pl.BlockSpec((tm, LANES), tab)
    names = ['q', 'kf', 'vf', 'kb', 'vb', 'sag', 'rq', 'rk', 'rv', 'srg', 'su', 'ssg']
    dts = [BF16, F32, F32, BF16, BF16, BF16, ret_dtype, ret_dtype, ret_dtype, BF16, BF16, BF16]
    outs = pl.pallas_call(
        _inproj_kernel,
        grid=(nblk, N_MIX // BRANCH_W),
        in_specs=[pl.BlockSpec((tm, D_MODEL), row),
                  pl.BlockSpec((1, D_MODEL), const),
                  pl.BlockSpec((D_MODEL, BRANCH_W), lambda i, j: (0, j)),
                  tspec, tspec, tspec, tspec,
                  pl.BlockSpec((1, LANES), const), pl.BlockSpec((1, LANES), const),
                  pl.BlockSpec((LANES, LANES), const)],
        out_specs=[sec] * len(names),
        out_shape=[jax.ShapeDtypeStruct((n, BRANCH_W), dt) for dt in dts],
        scratch_shapes=[pltpu.VMEM((tm, D_MODEL), BF16)],
        compiler_params=_cparams(("arbitrary", "arbitrary")),
        name="inproj",
    )(x, p['norm_gain'], p['w_mix'], tabs['cosa'], tabs['sina'], tabs['cosr'], tabs['sinr'],
      p['q_gain'], p['k_gain'], p['grp'])
    return dict(zip(names, outs))


def _diff_lambda(lq1, lk1, lq2, lk2, lam_init):
    return (jnp.exp(jnp.sum(lq1 * lk1, axis=-1, keepdims=True))
            - jnp.exp(jnp.sum(lq2 * lk2, axis=-1, keepdims=True)) + lam_init)


def _attn_finish(o, gain, sag, lam_init):
    ms = jnp.mean(o * o, axis=-1, keepdims=True)
    return o * lax.rsqrt(ms + EPS) * gain * (1.0 - lam_init) * sag


def _attn_prompt_kernel(qi_ref, ki_ref, q_ref, k_ref, v_ref, sag_ref, lq1_ref, lk1_ref, lq2_ref, lk2_ref,
                        gain_ref, o_ref, qs_scr, m_scr, l_scr, acc_scr, *, lam_init, tq):
    t = pl.program_id(1)
    qi = qi_ref[t]
    ki = ki_ref[t]
    lane = lax.broadcasted_iota(jnp.int32, (tq, LANES), 1)

    @pl.when(ki == 0)
    def _():
        for h in range(A_HEADS):
            qh = q_ref[:, h * LANES:(h + 1) * LANES]
            zero = jnp.zeros_like(qh)
            qs_scr[h, 0:tq, :] = jnp.where(lane < A_DH, qh, zero)
            qs_scr[h, tq:2 * tq, :] = jnp.where(lane >= A_DH, qh, zero)
        m_scr[...] = jnp.full(m_scr.shape, -jnp.inf, F32)
        l_scr[...] = jnp.zeros(l_scr.shape, F32)
        acc_scr[...] = jnp.zeros(acc_scr.shape, F32)

    def update(masked):
        for h in range(A_HEADS):
            kh = k_ref[:, h * LANES:(h + 1) * LANES]
            vh = v_ref[:, h * LANES:(h + 1) * LANES]
            s = lax.dot_general(qs_scr[h], kh, (((1,), (1,)), ((), ())), preferred_element_type=F32)
            if masked:
                r = lax.broadcasted_iota(jnp.int32, s.shape, 0)
                c = lax.broadcasted_iota(jnp.int32, s.shape, 1)
                r = jnp.where(r >= tq, r - tq, r)
                s = jnp.where(c <= r, s, -jnp.inf)
            m_prev = m_scr[h]
            m_new = jnp.maximum(m_prev, jnp.max(s, axis=-1, keepdims=True))
            alpha = jnp.exp2(m_prev - m_new)
            pexp = jnp.exp2(s - m_new)
            l_scr[h] = alpha * l_scr[h] + jnp.sum(pexp, axis=-1, keepdims=True)
            acc_scr[h] = alpha * acc_scr[h] + jnp.dot(pexp.astype(BF16), vh, preferred_element_type=F32)
            m_scr[h] = m_new

    @pl.when(ki < qi)
    def _():
        update(False)

    @pl.when(ki == qi)
    def _():
        update(True)
        lam = _diff_lambda(lq1_ref[...], lk1_ref[...], lq2_ref[...], lk2_ref[...], lam_init)
        for h in range(A_HEADS):
            acc = acc_scr[h]
            l = l_scr[h]
            o = acc[0:tq] / l[0:tq] - lam * (acc[tq:2 * tq] / l[tq:2 * tq])
            sl = slice(h * LANES, (h + 1) * LANES)
            o_ref[:, sl] = _attn_finish(o, gain_ref[:, sl], sag_ref[:, sl].astype(F32), lam_init).astype(BF16)


def _attn_prompt_shift_kernel(qi_ref, ki_ref, q_ref, k_ref, v_ref, sag_ref, lq1_ref, lk1_ref, lq2_ref, lk2_ref,
                              gain_ref, shift_ref, o_ref, qs_scr, acc_scr, *, lam_init, tq, rc):
    t = pl.program_id(1)
    qi = qi_ref[t]
    ki = ki_ref[t]
    lane = lax.broadcasted_iota(jnp.int32, (tq, LANES), 1)

    @pl.when(ki == 0)
    def _():
        for h in range(A_HEADS):
            qh = q_ref[:, h * LANES:(h + 1) * LANES]
            zero = jnp.zeros_like(qh)
            qs_scr[h, 0:tq, :] = jnp.where(lane < A_DH, qh, zero)
            qs_scr[h, tq:2 * tq, :] = jnp.where(lane >= A_DH, qh, zero)
        acc_scr[...] = jnp.zeros(acc_scr.shape, F32)

    shift = shift_ref[...]
    ones = jnp.ones((tq, LANES), BF16)

    def update(masked):
        for h in range(A_HEADS):
            kh = k_ref[:, h * LANES:(h + 1) * LANES]
            vaug = jnp.concatenate([v_ref[:, h * LANES:(h + 1) * LANES], ones], axis=1)
            for r0 in range(0, 2 * tq, rc):
                s = lax.dot_general(qs_scr[h, r0:r0 + rc, :], kh, (((1,), (1,)), ((), ())),
                                    preferred_element_type=F32)
                if masked:
                    r = lax.broadcasted_iota(jnp.int32, s.shape, 0) + (r0 % tq)
                    c = lax.broadcasted_iota(jnp.int32, s.shape, 1)
                    s = jnp.where(c <= r, s, -jnp.inf)
                pexp = jnp.exp2(s - shift).astype(BF16)
                acc_scr[h, r0:r0 + rc, :] += jnp.dot(pexp, vaug, preferred_element_type=F32)

    @pl.when(ki < qi)
    def _():
        update(False)

    @pl.when(ki == qi)
    def _():
        update(True)
        lam = _diff_lambda(lq1_ref[...], lk1_ref[...], lq2_ref[...], lk2_ref[...], lam_init)
        for h in range(A_HEADS):
            a1 = acc_scr[h, 0:tq, :]
            a2 = acc_scr[h, tq:2 * tq, :]
            o = a1[:, 0:LANES] / a1[:, LANES:2 * LANES] - lam * (a2[:, 0:LANES] / a2[:, LANES:2 * LANES])
            sl = slice(h * LANES, (h + 1) * LANES)
            o_ref[:, sl] = _attn_finish(o, gain_ref[:, sl], sag_ref[:, sl].astype(F32), lam_init).astype(BF16)


ATTN_SHIFT_LIMIT = 48.0


def _attn_prompt(mix, p, nb, seq, lam_init, tq, rc):
    n = nb * seq
    nq = seq // tq
    pairs = [(a, b) for a in range(nq) for b in range(a + 1)]
    qi = jnp.asarray(np.array([a for a, _ in pairs], np.int32))
    ki = jnp.asarray(np.array([b for _, b in pairs], np.int32))
    qmap = lambda b, t, qi, ki: (b * nq + qi[t], 0)
    kmap = lambda b, t, qi, ki: (b * nq + ki[t], 0)
    const = lambda b, t, qi, ki: (0, 0)
    blk = lambda m: pl.BlockSpec((tq, BRANCH_W), m)
    vec = pl.BlockSpec((1, A_DH), const)
    common = [blk(qmap), blk(kmap), blk(kmap), blk(qmap), vec, vec, vec, vec, pl.BlockSpec((1, BRANCH_W), const)]
    args = (qi, ki, mix['q'], mix['kb'], mix['vb'], mix['sag'], p['lam_q1'], p['lam_k1'], p['lam_q2'], p['lam_k2'],
            p['attn_gain'])
    out_shape = jax.ShapeDtypeStruct((n, BRANCH_W), BF16)

    def running_max(_):
        return pl.pallas_call(
            functools.partial(_attn_prompt_kernel, lam_init=lam_init, tq=tq),
            grid_spec=pltpu.PrefetchScalarGridSpec(
                num_scalar_prefetch=2, grid=(nb, len(pairs)), in_specs=common, out_specs=blk(qmap),
                scratch_shapes=[pltpu.VMEM((A_HEADS, 2 * tq, LANES), BF16),
                                pltpu.VMEM((A_HEADS, 2 * tq, 1), F32),
                                pltpu.VMEM((A_HEADS, 2 * tq, 1), F32),
                                pltpu.VMEM((A_HEADS, 2 * tq, LANES), F32)]),
            out_shape=out_shape,
            compiler_params=_cparams(("arbitrary", "arbitrary")),
            name="attn_prompt",
        )(*args)

    def fixed_shift(shift):
        return pl.pallas_call(
            functools.partial(_attn_prompt_shift_kernel, lam_init=lam_init, tq=tq, rc=rc),
            grid_spec=pltpu.PrefetchScalarGridSpec(
                num_scalar_prefetch=2, grid=(nb, len(pairs)),
                in_specs=common + [pl.BlockSpec((1, 1), const)], out_specs=blk(qmap),
                scratch_shapes=[pltpu.VMEM((A_HEADS, 2 * tq, LANES), BF16),
                                pltpu.VMEM((A_HEADS, 2 * tq, 2 * LANES), F32)]),
            out_shape=out_shape,
            compiler_params=_cparams(("arbitrary", "arbitrary")),
            name="attn_prompt_shift",
        )(*args, shift)

    bound = (A_DH ** 0.5) * jnp.max(jnp.abs(p['q_gain'])) * jnp.max(jnp.abs(p['k_gain']))
    shift = (bound * LOG2E).reshape(1, 1).astype(F32)
    return lax.cond(2.0 * bound <= ATTN_SHIFT_LIMIT, fixed_shift, running_max, shift)


def _attn_sample_kernel(pt_ref, *refs, lam_init, pages):
    kc_refs = refs[0:pages]
    vc_refs = refs[pages:2 * pages]
    (qt_ref, kn_ref, vn_ref, sag_ref, lq1_ref, lk1_ref, lq2_ref, lk2_ref, gain_ref,
     o_ref, m_scr, l_scr, acc_scr) = refs[2 * pages:]
    s_id = pl.program_id(1)
    n_maps = 2 * A_HEADS
    qt = qt_ref[0]

    @pl.when(s_id == 0)
    def _():
        m_scr[...] = jnp.full(m_scr.shape, -jnp.inf, F32)
        l_scr[...] = jnp.zeros(l_scr.shape, F32)
        acc_scr[...] = jnp.zeros(acc_scr.shape, F32)

    kcat = jnp.concatenate([r[...].astype(BF16) for r in kc_refs], axis=0)
    vcat = jnp.concatenate([r[...].astype(BF16) for r in vc_refs], axis=0)
    s = lax.dot_general(qt.astype(BF16), kcat, (((1,), (1,)), ((), ())), preferred_element_type=F32)
    row = lax.broadcasted_iota(jnp.int32, s.shape, 0)
    col = lax.broadcasted_iota(jnp.int32, s.shape, 1)
    s = jnp.where(col % A_HEADS == row // 2, s, -jnp.inf)
    m_prev = m_scr[...]
    m_new = jnp.maximum(m_prev, jnp.max(s, axis=-1, keepdims=True))
    alpha = jnp.exp2(m_prev - m_new)
    pexp = jnp.exp2(s - m_new)
    l_scr[...] = alpha * l_scr[...] + jnp.sum(pexp, axis=-1, keepdims=True)
    acc_scr[...] = alpha * acc_scr[...] + jnp.dot(pexp.astype(BF16), vcat, preferred_element_type=F32)
    m_scr[...] = m_new

    @pl.when(s_id == pl.num_programs(1) - 1)
    def _():
        s_new = jnp.sum(qt * kn_ref[0], axis=-1, keepdims=True)
        m_prev = m_scr[...]
        m_fin = jnp.maximum(m_prev, s_new)
        alpha = jnp.exp2(m_prev - m_fin)
        p_new = jnp.exp2(s_new - m_fin)
        l = alpha * l_scr[...] + p_new
        acc = alpha * acc_scr[...] + p_new * vn_ref[0]
        lam = _diff_lambda(lq1_ref[...], lk1_ref[...], lq2_ref[...], lk2_ref[...], lam_init)
        r8 = lax.broadcasted_iota(jnp.int32, (n_maps, 1), 0)
        on = acc * (jnp.where(r8 % 2 == 0, 1.0, -lam) / l)
        for h in range(A_HEADS):
            sl = slice(h * LANES, (h + 1) * LANES)
            o = on[2 * h:2 * h + 1] + on[2 * h + 1:2 * h + 2]
            o_ref[0, :, sl] = _attn_finish(o, gain_ref[:, sl], sag_ref[0, :, sl].astype(F32),
                                           lam_init).astype(BF16)


def _attn_sample(mix, p, cache_k, cache_v, page_table, layer, lam_init, pages):
    nd, n_pages = page_table.shape
    steps = n_pages // pages
    pool = cache_k.shape[1]
    rows_pp = PAGE_SIZE * A_HEADS
    ck = cache_k.reshape(cache_k.shape[0], pool, rows_pp, A_DV)
    cv = cache_v.reshape(cache_v.shape[0], pool, rows_pp, A_DV)

    def page_spec(i):
        return pl.BlockSpec((None, None, rows_pp, A_DV),
                            lambda b, s, pt: (layer, pt[b * n_pages + s * pages + i], 0, 0))

    n_maps = 2 * A_HEADS
    tok = pl.BlockSpec((1, 1, BRANCH_W), lambda b, s, pt: (b, 0, 0))
    tok8 = pl.BlockSpec((1, n_maps, LANES), lambda b, s, pt: (b, 0, 0))
    const = lambda b, s, pt: (0, 0)
    vec = pl.BlockSpec((1, A_DH), const)
    half = (jnp.arange(LANES)[None, :] // A_DH == jnp.arange(2)[:, None]).astype(F32)
    qt = (mix['q'].astype(F32).reshape(nd, A_HEADS, 1, LANES) * half).reshape(nd, n_maps, LANES)
    rep = lambda a: jnp.repeat(a.reshape(nd, A_HEADS, LANES), 2, axis=1)
    out = pl.pallas_call(
        functools.partial(_attn_sample_kernel, lam_init=lam_init, pages=pages),
        grid_spec=pltpu.PrefetchScalarGridSpec(
            num_scalar_prefetch=1,
            grid=(nd, steps),
            in_specs=[page_spec(i) for i in range(pages)] * 2
                     + [tok8, tok8, tok8, tok, vec, vec, vec, vec, pl.BlockSpec((1, BRANCH_W), const)],
            out_specs=tok,
            scratch_shapes=[pltpu.VMEM((n_maps, 1), F32), pltpu.VMEM((n_maps, 1), F32),
                            pltpu.VMEM((n_maps, LANES), F32)]),
        out_shape=jax.ShapeDtypeStruct((nd, 1, BRANCH_W), BF16),
        compiler_params=_cparams(("arbitrary", "arbitrary")),
        name="attn_sample",
    )(page_table.reshape(-1), *([ck] * pages), *([cv] * pages),
      qt, rep(mix['kf']), rep(mix['vf']), mix['sag'].reshape(nd, 1, BRANCH_W),
      p['lam_q1'], p['lam_k1'], p['lam_q2'], p['lam_k2'], p['attn_gain'])
    return out.reshape(nd, BRANCH_W)


def _ret_finish(o, gain, srg):
    ms = jnp.mean(o * o, axis=-1, keepdims=True)
    return o * lax.rsqrt(ms + EPS) * gain * srg


def _ret_prompt_kernel(q_ref, k_ref, v_ref, srg_ref, gain_ref, o_ref, st_ref, s_scr, dm_scr, qd_scr, kd_scr, *, c):
    ci = pl.program_id(1)

    @pl.when(ci == 0)
    def _():
        s_scr[...] = jnp.zeros(s_scr.shape, F32)
        i = lax.broadcasted_iota(jnp.int32, (c, c), 0)
        jj = lax.broadcasted_iota(jnp.int32, (c, c), 1)
        d = (i - jj).astype(F32)
        r = lax.broadcasted_iota(jnp.int32, (c, LANES), 0).astype(F32)
        for h in range(R_HEADS):
            lg = RET_LOG_G[h]
            dm_scr[h] = jnp.where(i >= jj, jnp.exp(d * lg), 0.0)
            qd_scr[h] = jnp.exp((r + 1.0) * lg)
            kd_scr[h] = jnp.exp((c - 1.0 - r) * lg)

    for h in range(R_HEADS):
        sl = slice(h * LANES, (h + 1) * LANES)
        q = q_ref[:, sl]
        k = k_ref[:, sl]
        v = v_ref[:, sl]
        att = lax.dot_general(q, k, (((1,), (1,)), ((), ())), preferred_element_type=F32) * dm_scr[h]
        s_old = s_scr[h]
        o = (jnp.dot(att.astype(BF16), v, preferred_element_type=F32)
             + jnp.dot((q.astype(F32) * qd_scr[h]).astype(BF16), s_old.astype(BF16), preferred_element_type=F32))
        kd = (k.astype(F32) * kd_scr[h]).astype(BF16)
        s_new = math.exp(c * RET_LOG_G[h]) * s_old + lax.dot_general(
            kd, v, (((0,), (0,)), ((), ())), preferred_element_type=F32)
        s_scr[h] = s_new
        o_ref[:, sl] = _ret_finish(o, gain_ref[:, sl], srg_ref[:, sl].astype(F32)).astype(BF16)

    @pl.when(ci == pl.num_programs(1) - 1)
    def _():
        st_ref[0] = s_scr[...]


def _ret_prompt(mix, p, nb, seq, c):
    n = nb * seq
    nc = seq // c
    row = lambda b, i: (b * nc + i, 0)
    blk = pl.BlockSpec((c, BRANCH_W), row)
    return pl.pallas_call(
        functools.partial(_ret_prompt_kernel, c=c),
        grid=(nb, nc),
        in_specs=[blk, blk, blk, blk, pl.BlockSpec((1, BRANCH_W), lambda b, i: (0, 0))],
        out_specs=[blk, pl.BlockSpec((1, R_HEADS, R_DK, R_DV), lambda b, i: (b, 0, 0, 0))],
        out_shape=[jax.ShapeDtypeStruct((n, BRANCH_W), BF16),
                   jax.ShapeDtypeStruct((nb, R_HEADS, R_DK, R_DV), F32)],
        scratch_shapes=[pltpu.VMEM((R_HEADS, R_DK, R_DV), F32), pltpu.VMEM((R_HEADS, c, c), F32),
                        pltpu.VMEM((R_HEADS, c, LANES), F32), pltpu.VMEM((R_HEADS, c, LANES), F32)],
        compiler_params=_cparams(("arbitrary", "arbitrary")),
        name="ret_prompt",
    )(mix['rq'], mix['rk'], mix['rv'], mix['srg'], p['ret_gain'])


def _ret_sample_kernel(q_ref, k_ref, v_ref, srg_ref, gain_ref, s0_ref, o_ref, st_ref, *, nb):
    for b in range(nb):
        for h in range(R_HEADS):
            sl = slice(h * LANES, (h + 1) * LANES)
            g = math.exp(RET_LOG_G[h])
            q = q_ref[b, :, sl]
            k = k_ref[b, :, sl]
            v = v_ref[b, :, sl]
            s0 = s0_ref[b, h]
            qcol = jnp.transpose(jnp.broadcast_to(q, (SUBLANES, LANES)))[:, 0:1]
            kcol = jnp.transpose(jnp.broadcast_to(k, (SUBLANES, LANES)))[:, 0:1]
            o = jnp.sum(q * k, axis=-1, keepdims=True) * v + g * jnp.sum(qcol * s0, axis=0, keepdims=True)
            st_ref[b, h] = g * s0 + kcol * v
            o_ref[b, :, sl] = _ret_finish(o, gain_ref[:, sl], srg_ref[b, :, sl].astype(F32)).astype(BF16)


def _ret_sample(mix, p, s0, nb_blk):
    nd = s0.shape[0]
    r3 = lambda a: a.reshape(nd, 1, BRANCH_W)
    tok = pl.BlockSpec((nb_blk, 1, BRANCH_W), lambda i: (i, 0, 0))
    st = pl.BlockSpec((nb_blk, R_HEADS, R_DK, R_DV), lambda i: (i, 0, 0, 0))
    o, s1 = pl.pallas_call(
        functools.partial(_ret_sample_kernel, nb=nb_blk),
        grid=(nd // nb_blk,),
        in_specs=[tok, tok, tok, tok, pl.BlockSpec((1, BRANCH_W), lambda i: (0, 0)), st],
        out_specs=[tok, st],
        out_shape=[jax.ShapeDtypeStruct((nd, 1, BRANCH_W), BF16),
                   jax.ShapeDtypeStruct((nd, R_HEADS, R_DK, R_DV), F32)],
        compiler_params=_cparams(("arbitrary",)),
        name="ret_sample",
    )(r3(mix['rq']), r3(mix['rk']), r3(mix['rv']), r3(mix['srg']), p['ret_gain'], s0)
    return o.reshape(nd, BRANCH_W), s1


S5_LANE_BLK = 512


def _cmul_add(ar, ai, hr, hi, br, bi):
    return ar * hr - ai * hi + br, ar * hi + ai * hr + bi


def _s5_prompt_kernel(u_ref, bmat_ref, cmat_ref, d_ref, ar_ref, ai_ref, atr_ref, ati_ref,
                      y_ref, hre_ref, him_ref, up_scr, bu_scr, yp_scr, car_scr, *, tm):
    it = pl.program_id(1)
    tp = tm // SUBLANES

    @pl.when(it == 0)
    def _():
        car_scr[...] = jnp.zeros(car_scr.shape, F32)

    u = u_ref[...].astype(F32)
    nlb = BRANCH_W // LANES
    for j in range(SUBLANES):
        for kb in range(nlb):
            up_scr[kb, pl.ds(j, tp, stride=SUBLANES), :] = u[j * tp:(j + 1) * tp, kb * LANES:(kb + 1) * LANES]
    up = jnp.concatenate([up_scr[kb] for kb in range(nlb)], axis=1)
    bu_scr[...] = jnp.dot(up.astype(BF16), bmat_ref[...], preferred_element_type=F32)

    for cb in range(S_CH // S5_LANE_BLK):
        re = slice(cb * S5_LANE_BLK, (cb + 1) * S5_LANE_BLK)
        im = slice(S_CH + cb * S5_LANE_BLK, S_CH + (cb + 1) * S5_LANE_BLK)
        ar = jnp.broadcast_to(ar_ref[:, re], (SUBLANES, S5_LANE_BLK))
        ai = jnp.broadcast_to(ai_ref[:, re], (SUBLANES, S5_LANE_BLK))

        def local_step(i, carry):
            hr, hi = carry
            r0 = pl.multiple_of(i * SUBLANES, SUBLANES)
            return _cmul_add(ar, ai, hr, hi, bu_scr[pl.ds(r0, SUBLANES), re], bu_scr[pl.ds(r0, SUBLANES), im])

        zero = jnp.zeros((SUBLANES, S5_LANE_BLK), F32)
        er, ei = lax.fori_loop(0, tp, local_step, (zero, zero))

        atr = atr_ref[:, re]
        ati = ati_ref[:, re]
        cr = car_scr[0:1, re]
        ci = car_scr[0:1, im]
        rows_r, rows_i = [cr], [ci]
        for j in range(1, SUBLANES):
            cr, ci = _cmul_add(atr, ati, cr, ci, er[j - 1:j], ei[j - 1:j])
            rows_r.append(cr)
            rows_i.append(ci)
        cr, ci = _cmul_add(atr, ati, cr, ci, er[SUBLANES - 1:SUBLANES], ei[SUBLANES - 1:SUBLANES])
        car_scr[0:1, re] = cr
        car_scr[0:1, im] = ci
        sr = jnp.concatenate(rows_r, axis=0)
        si = jnp.concatenate(rows_i, axis=0)

        def full_step(i, carry):
            hr, hi = carry
            r0 = pl.multiple_of(i * SUBLANES, SUBLANES)
            hr, hi = _cmul_add(ar, ai, hr, hi, bu_scr[pl.ds(r0, SUBLANES), re], bu_scr[pl.ds(r0, SUBLANES), im])
            bu_scr[pl.ds(r0, SUBLANES), re] = hr
            bu_scr[pl.ds(r0, SUBLANES), im] = hi
            return hr, hi

        lax.fori_loop(0, tp, full_step, (sr, si))

    yp = jnp.dot(bu_scr[...].astype(BF16), cmat_ref[...], preferred_element_type=F32) + d_ref[...] * up
    for kb in range(nlb):
        yp_scr[kb] = yp[:, kb * LANES:(kb + 1) * LANES]
    for j in range(SUBLANES):
        for kb in range(nlb):
            y_ref[j * tp:(j + 1) * tp, kb * LANES:(kb + 1) * LANES] = _gelu_tanh(
                yp_scr[kb, pl.ds(j, tp, stride=SUBLANES), :]).astype(BF16)

    @pl.when(it == pl.num_programs(1) - 1)
    def _():
        hre_ref[0] = car_scr[0:1, 0:S_CH]
        him_ref[0] = car_scr[0:1, S_CH:2 * S_CH]


def _s5_prompt(mix, s5, nb, seq, tm):
    n = nb * seq
    nt = seq // tm
    row = lambda b, i: (b * nt + i, 0)
    const = lambda b, i: (0, 0)
    blk = pl.BlockSpec((tm, BRANCH_W), row)
    vec = pl.BlockSpec((1, S_CH), const)
    st = pl.BlockSpec((1, 1, S_CH), lambda b, i: (b, 0, 0))
    y, hre, him = pl.pallas_call(
        functools.partial(_s5_prompt_kernel, tm=tm),
        grid=(nb, nt),
        in_specs=[blk, pl.BlockSpec((BRANCH_W, 2 * S_CH), const), pl.BlockSpec((2 * S_CH, BRANCH_W), const),
                  pl.BlockSpec((1, BRANCH_W), const), vec, vec, vec, vec],
        out_specs=[blk, st, st],
        out_shape=[jax.ShapeDtypeStruct((n, BRANCH_W), BF16),
                   jax.ShapeDtypeStruct((nb, 1, S_CH), F32), jax.ShapeDtypeStruct((nb, 1, S_CH), F32)],
        scratch_shapes=[pltpu.VMEM((BRANCH_W // LANES, tm, LANES), F32), pltpu.VMEM((tm, 2 * S_CH), F32),
                        pltpu.VMEM((BRANCH_W // LANES, tm, LANES), F32), pltpu.VMEM((SUBLANES, 2 * S_CH), F32)],
        compiler_params=_cparams(("arbitrary", "arbitrary")),
        name="s5_prompt",
    )(mix['su'], s5['bmat'], s5['cmat'], s5['d'], s5['a_re'], s5['a_im'], s5['at_re'][tm // SUBLANES],
      s5['at_im'][tm // SUBLANES])
    return y, hre.reshape(nb, S_GROUPS, S_STATE), him.reshape(nb, S_GROUPS, S_STATE)


def _s5_sample_kernel(u_ref, bmat_ref, cmat_ref, d_ref, ar_ref, ai_ref, h0r_ref, h0i_ref, y_ref, hr_ref, hi_ref):
    u = u_ref[...]
    bu = jnp.dot(u, bmat_ref[...], preferred_element_type=F32)
    hr, hi = _cmul_add(ar_ref[...], ai_ref[...], h0r_ref[...], h0i_ref[...], bu[:, 0:S_CH], bu[:, S_CH:2 * S_CH])
    hr_ref[...] = hr
    hi_ref[...] = hi
    h = jnp.concatenate([hr, hi], axis=-1).astype(BF16)
    y = jnp.dot(h, cmat_ref[...], preferred_element_type=F32) + d_ref[...] * u.astype(F32)
    y_ref[...] = _gelu_tanh(y).astype(BF16)


def _s5_sample(mix, s5, h0_re, h0_im):
    nd = h0_re.shape[0]
    y, hr, hi = pl.pallas_call(
        _s5_sample_kernel,
        out_shape=[jax.ShapeDtypeStruct((nd, BRANCH_W), BF16),
                   jax.ShapeDtypeStruct((nd, S_CH), F32), jax.ShapeDtypeStruct((nd, S_CH), F32)],
        compiler_params=pltpu.CompilerParams(vmem_limit_bytes=VMEM_LIMIT),
        name="s5_sample",
    )(mix['su'], s5['bmat'], s5['cmat'], s5['d'], s5['a_re'], s5['a_im'],
      h0_re.reshape(nd, S_CH), h0_im.reshape(nd, S_CH))
    return y, hr.reshape(nd, S_GROUPS, S_STATE), hi.reshape(nd, S_GROUPS, S_STATE)


def _merge_kernel(x_ref, g_ref, wmg_ref, oa_ref, or_ref, ys_ref, ssg_ref, wba_ref, wbr_ref, wbs_ref,
                  wglu_ref, bglu_ref, wout_ref, out_ref):
    x = x_ref[...]
    ms = jnp.mean(x * x, axis=-1, keepdims=True)
    h = (x * lax.rsqrt(ms + EPS) * g_ref[...]).astype(BF16)
    gates = jax.nn.sigmoid(jnp.dot(h, wmg_ref[...], preferred_element_type=F32))
    ya = jnp.dot(oa_ref[...], wba_ref[...], preferred_element_type=F32)
    yr = jnp.dot(or_ref[...], wbr_ref[...], preferred_element_type=F32)
    glu = jnp.dot(ys_ref[...], wglu_ref[...], preferred_element_type=F32) + bglu_ref[...]
    s5 = glu[:, 0:BRANCH_W] * jax.nn.sigmoid(glu[:, BRANCH_W:2 * BRANCH_W]) * ssg_ref[...].astype(F32)
    y5 = jnp.dot(s5.astype(BF16), wbs_ref[...], preferred_element_type=F32)
    merged = (gates[:, 0:D_MODEL] * ya + gates[:, D_MODEL:2 * D_MODEL] * yr
              + gates[:, 2 * D_MODEL:3 * D_MODEL] * y5)
    out_ref[...] = x + jnp.dot(merged.astype(BF16), wout_ref[...], preferred_element_type=F32)


def _merge(x, o_attn, o_ret, y_s5, ssg, p, tm):
    n = x.shape[0]
    row = lambda i: (i, 0)
    const = lambda i: (0, 0)
    half = pl.BlockSpec((tm, BRANCH_W), row)
    full = lambda a: pl.BlockSpec(a.shape, const)
    ws = [p['w_br_attn'], p['w_br_ret'], p['w_br_s5'], p['w_glu'], p['b_glu'], p['w_out']]
    return pl.pallas_call(
        _merge_kernel,
        grid=(n // tm,),
        in_specs=[pl.BlockSpec((tm, D_MODEL), row), full(p['norm_gain']), full(p['w_mg']),
                  half, half, half, half] + [full(w) for w in ws],
        out_specs=pl.BlockSpec((tm, D_MODEL), row),
        out_shape=jax.ShapeDtypeStruct((n, D_MODEL), F32),
        compiler_params=_cparams(("arbitrary",)),
        name="merge",
    )(x, p['norm_gain'], p['w_mg'], o_attn, o_ret, y_s5, ssg, *ws)


def _rot_tables(pos, rows):
    pos = pos.astype(F32)[:, None]

    def tab(half, reps):
        inv = ROPE_THETA ** (-jnp.arange(half, dtype=F32) / half)
        ang = pos * inv[None, :]
        cos, sin = jnp.cos(ang), jnp.sin(ang)
        c = jnp.tile(jnp.concatenate([cos, cos], axis=-1), (1, reps))
        s = jnp.tile(jnp.concatenate([-sin, sin], axis=-1), (1, reps))
        if c.shape[0] != rows:
            c, s = jnp.broadcast_to(c, (rows, LANES)), jnp.broadcast_to(s, (rows, LANES))
        return c, s

    cosa, sina = tab(A_DH // 2, LANES // A_DH)
    cosr, sinr = tab(R_DK // 2, 1)
    return dict(cosa=cosa, sina=sina, cosr=cosr, sinr=sinr)


def _s5_params(lam_re, lam_im, log_dt, b_re, b_im, c_re, c_im, d_skip, powers):
    dt = jnp.exp(log_dt.astype(F32))[:, None]
    lre, lim = lam_re.astype(F32), lam_im.astype(F32)
    mag = jnp.exp(lre * dt)
    ab_re, ab_im = mag * jnp.cos(lim * dt), mag * jnp.sin(lim * dt)
    den = lre * lre + lim * lim
    nre = ab_re - 1.0
    cre = (nre * lre + ab_im * lim) / den
    cim = (ab_im * lre - nre * lim) / den
    b_re, b_im = b_re.astype(F32), b_im.astype(F32)
    bb_re = cre[..., None] * b_re - cim[..., None] * b_im
    bb_im = cre[..., None] * b_im + cim[..., None] * b_re
    eye = jnp.eye(S_GROUPS, dtype=F32)

    def in_mat(bb):
        return jnp.einsum('gpc,gh->gchp', bb, eye).reshape(BRANCH_W, S_CH)

    def out_mat(cc):
        return jnp.einsum('gcp,gh->gphc', cc.astype(F32), eye).reshape(S_CH, BRANCH_W)

    bmat = jnp.concatenate([in_mat(bb_re), in_mat(bb_im)], axis=1).astype(BF16)
    cmat = jnp.concatenate([out_mat(c_re), -out_mat(c_im)], axis=0).astype(BF16)
    a_re, a_im = ab_re.reshape(1, S_CH), ab_im.reshape(1, S_CH)
    at_re, at_im = {1: a_re}, {1: a_im}
    pr, pi, k = a_re, a_im, 1
    while k < max(powers):
        pr, pi, k = pr * pr - pi * pi, 2.0 * pr * pi, 2 * k
        at_re[k], at_im[k] = pr, pi
    return dict(bmat=bmat, cmat=cmat, d=d_skip.astype(F32).reshape(1, BRANCH_W), a_re=a_re, a_im=a_im,
                at_re=at_re, at_im=at_im)


TM_PROJ = 512
TQ_ATTN = 512
RC_ATTN = 256
RET_CHUNK = 256
TM_S5 = 512
TM_MERGE = 512
SAMPLE_PAGES = 8
RET_SAMPLE_BLK = 8


def kernel(x_prompt, x_sample, cache_k, cache_v, state_ret, state_s5_re, state_s5_im, page_table, norm_gain, w_in, q_norm_gain, k_norm_gain, lam_q1, lam_k1, lam_q2, lam_k2, attn_out_gain, ret_out_gain, s5_lam_re, s5_lam_im, s5_log_dt, s5_b_re, s5_b_im, s5_c_re, s5_c_im, s5_d, w_glu, b_glu, w_br_attn, w_br_ret, w_br_s5, w_out):
    depth = w_in.shape[0]
    nb, seq, _ = x_prompt.shape
    nd, dseq, _ = x_sample.shape
    assert dseq == 1, "the sample group is a single-token step"
    n_pages = page_table.shape[1]
    past_len = n_pages * PAGE_SIZE
    tm_proj, tq, c_ret = min(TM_PROJ, seq), min(TQ_ATTN, seq), min(RET_CHUNK, seq)
    tm_s5, tm_merge = min(TM_S5, seq), min(TM_MERGE, seq)
    pages = min(SAMPLE_PAGES, n_pages)
    assert seq % tm_proj == 0 and seq % tq == 0 and seq % c_ret == 0 and seq % tm_s5 == 0 and seq % tm_merge == 0
    assert n_pages % pages == 0 and nd % RET_SAMPLE_BLK == 0

    tabs_p = _rot_tables(jnp.arange(seq), seq)
    tabs_s = _rot_tables(past_len + jnp.arange(1), nd)
    grp = (jnp.arange(LANES)[:, None] // A_DH == jnp.arange(LANES)[None, :] // A_DH).astype(BF16)

    yp = x_prompt.reshape(nb * seq, D_MODEL)
    ys = x_sample.reshape(nd, D_MODEL)
    outs = {k: [] for k in ('kp', 'vp', 'rp', 'spr', 'spi', 'ks', 'vs', 'rs', 'ssr', 'ssi')}
    for l in range(depth):
        lam_init = 0.8 - 0.6 * math.exp(-0.3 * l)
        w_l = w_in[l].astype(BF16)
        p = dict(
            norm_gain=norm_gain[l].reshape(1, D_MODEL), w_mix=w_l[:, :N_MIX], w_mg=w_l[:, N_MIX:], grp=grp,
            q_gain=jnp.tile(q_norm_gain[l], LANES // A_DH).reshape(1, LANES),
            k_gain=jnp.tile(k_norm_gain[l], LANES // A_DH).reshape(1, LANES),
            lam_q1=lam_q1[l].reshape(1, A_DH), lam_k1=lam_k1[l].reshape(1, A_DH),
            lam_q2=lam_q2[l].reshape(1, A_DH), lam_k2=lam_k2[l].reshape(1, A_DH),
            attn_gain=attn_out_gain[l].reshape(1, BRANCH_W), ret_gain=ret_out_gain[l].reshape(1, BRANCH_W),
            w_glu=w_glu[l].astype(BF16), b_glu=b_glu[l].reshape(1, 2 * BRANCH_W),
            w_br_attn=w_br_attn[l].astype(BF16), w_br_ret=w_br_ret[l].astype(BF16),
            w_br_s5=w_br_s5[l].astype(BF16), w_out=w_out[l].astype(BF16))
        s5 = _s5_params(s5_lam_re[l], s5_lam_im[l], s5_log_dt[l], s5_b_re[l], s5_b_im[l], s5_c_re[l], s5_c_im[l],
                        s5_d[l], (tm_s5 // SUBLANES,))

        mix = _inproj(yp, p, tabs_p, tm_proj, seq, BF16)
        o_attn = _attn_prompt(mix, p, nb, seq, lam_init, tq, min(RC_ATTN, tq))
        o_ret, ret_state = _ret_prompt(mix, p, nb, seq, c_ret)
        y_s5, s_re, s_im = _s5_prompt(mix, s5, nb, seq, tm_s5)
        yp = _merge(yp, o_attn, o_ret, y_s5, mix['ssg'], p, tm_merge)
        outs['kp'].append(mix['kf'].reshape(nb, seq, A_HEADS, A_DV))
        outs['vp'].append(mix['vf'].reshape(nb, seq, A_HEADS, A_DV))
        outs['rp'].append(ret_state)
        outs['spr'].append(s_re)
        outs['spi'].append(s_im)

        mix = _inproj(ys, p, tabs_s, nd, nd, F32)
        o_attn = _attn_sample(mix, p, cache_k, cache_v, page_table, l, lam_init, pages)
        o_ret, ret_state = _ret_sample(mix, p, state_ret[l], RET_SAMPLE_BLK)
        y_s5, s_re, s_im = _s5_sample(mix, s5, state_s5_re[l], state_s5_im[l])
        ys = _merge(ys, o_attn, o_ret, y_s5, mix['ssg'], p, nd)
        outs['ks'].append(mix['kf'].reshape(nd, 1, A_HEADS, A_DV))
        outs['vs'].append(mix['vf'].reshape(nd, 1, A_HEADS, A_DV))
        outs['rs'].append(ret_state)
        outs['ssr'].append(s_re)
        outs['ssi'].append(s_im)

    st = lambda k: jnp.stack(outs[k])
    return (yp.reshape(nb, seq, D_MODEL), ys.reshape(nd, 1, D_MODEL),
            st('kp'), st('vp'), st('rp'), st('spr'), st('spi'),
            st('ks'), st('vs'), st('rs'), st('ssr'), st('ssi'))
```

```python
import functools
import math

import jax
import jax.numpy as jnp
import numpy as np
from jax import lax
from jax.experimental import pallas as pl
from jax.experimental.pallas import tpu as pltpu

F32 = jnp.float32
BF16 = jnp.bfloat16

LANES = 128
SUBLANES = 8
VMEM_LIMIT = 56 * 1024 * 1024

D_MODEL = 1024
BRANCH_W = D_MODEL // 2
A_HEADS = 4
A_DH = BRANCH_W // (2 * A_HEADS)
A_DV = 2 * A_DH
R_HEADS = 4
R_DK = BRANCH_W // R_HEADS
R_DV = BRANCH_W // R_HEADS
S_GROUP = 16
S_GROUPS = BRANCH_W // S_GROUP
S_STATE = 64
S_CH = S_GROUPS * S_STATE
N_BRANCH = 3
N_MIX = 10 * BRANCH_W
PAGE_SIZE = 128
ROPE_THETA = 10000.0
EPS = 1e-6
LOG2E = math.log2(math.e)
RET_LOG_G = tuple(math.log1p(-2.0 ** (-5.0 - h)) for h in range(R_HEADS))

SEC_AQ, SEC_AK, SEC_AV, SEC_AG, SEC_RQ, SEC_RK, SEC_RV, SEC_RG, SEC_SU, SEC_SG = range(10)


def _cparams(sem):
    return pltpu.CompilerParams(dimension_semantics=sem, vmem_limit_bytes=VMEM_LIMIT)


def _silu(x):
    return x * jax.nn.sigmoid(x)


def _gelu_tanh(x):
    return 0.5 * x * (1.0 + jnp.tanh(math.sqrt(2.0 / math.pi) * (x + 0.044715 * x * x * x)))


def _split_dot(x, w):
    hi = x.astype(BF16)
    lo = (x - hi.astype(F32)).astype(BF16)
    return (jnp.dot(hi, w, preferred_element_type=F32) + jnp.dot(lo, w, preferred_element_type=F32))


def _inproj_kernel(x_ref, g_ref, w_ref, cosa_ref, sina_ref, cosr_ref, sinr_ref, qg_ref, kg_ref, grp_ref,
                   q_ref, kf_ref, vf_ref, kb_ref, vb_ref, sag_ref, rq_ref, rk_ref, rv_ref, srg_ref,
                   su_ref, ssg_ref):
    x = x_ref[...]
    tm = x.shape[0]
    ms = jnp.mean(x * x, axis=-1, keepdims=True)
    hn = (x * lax.rsqrt(ms + EPS) * g_ref[...]).astype(BF16)
    lane = lax.broadcasted_iota(jnp.int32, (tm, LANES), 1)

    def section(s):
        return jnp.dot(hn, w_ref[:, s * BRANCH_W:(s + 1) * BRANCH_W], preferred_element_type=F32)

    def head(t, h):
        return t[:, h * LANES:(h + 1) * LANES]

    def qk_norm_rot(zh, gain):
        ss = _split_dot(zh * zh, grp_ref[...])
        y = zh * lax.rsqrt(ss * (1.0 / A_DH) + EPS) * gain
        half = A_DH // 2
        swapped = jnp.where((lane & half) == 0, pltpu.roll(y, LANES - half, 1), pltpu.roll(y, half, 1))
        return y * cosa_ref[...] + swapped * sina_ref[...]

    def ret_rot(zh):
        return zh * cosr_ref[...] + pltpu.roll(zh, R_DK // 2, 1) * sinr_ref[...]

    def head_rows(h):
        return pl.ds(h, tm, stride=A_HEADS)

    z = section(SEC_AQ)
    for h in range(A_HEADS):
        y = qk_norm_rot(head(z, h), qg_ref[...]) * (A_DH ** -0.5 * LOG2E)
        q_ref[:, h * LANES:(h + 1) * LANES] = y.astype(BF16)

    z = section(SEC_AK)
    for h in range(A_HEADS):
        y = qk_norm_rot(head(z, h), kg_ref[...])
        kf_ref[head_rows(h), :] = y
        kb_ref[:, h * LANES:(h + 1) * LANES] = y.astype(BF16)

    z = section(SEC_AV)
    for h in range(A_HEADS):
        vf_ref[head_rows(h), :] = head(z, h)
    vb_ref[...] = z.astype(BF16)

    sag_ref[...] = _silu(section(SEC_AG)).astype(BF16)

    z = section(SEC_RQ)
    for h in range(R_HEADS):
        rq_ref[:, h * LANES:(h + 1) * LANES] = ret_rot(head(z, h)).astype(rq_ref.dtype)

    z = section(SEC_RK)
    for h in range(R_HEADS):
        rk_ref[:, h * LANES:(h + 1) * LANES] = (ret_rot(head(z, h)) * (R_DK ** -0.5)).astype(rk_ref.dtype)

    rv_ref[...] = section(SEC_RV).astype(rv_ref.dtype)
    srg_ref[...] = _silu(section(SEC_RG)).astype(BF16)
    su_ref[...] = section(SEC_SU).astype(BF16)
    ssg_ref[...] = _silu(section(SEC_SG)).astype(BF16)


def _inproj(x, p, tabs, tm, rows_per_seq, ret_dtype):
    n = x.shape[0]
    nblk = n // tm
    tab_blocks = rows_per_seq // tm if rows_per_seq >= tm else 1
    row = lambda i: (i, 0)
    tab = lambda i: (i % tab_blocks, 0)
    const = lambda i: (0, 0)
    sec = pl.BlockSpec((tm, BRANCH_W), row)
    hrow = pl.BlockSpec((tm * A_HEADS, A_DV), row)
    tspec = pl.BlockSpec((tm, LANES), tab)
    names = ['q', 'kf', 'vf', 'kb', 'vb', 'sag', 'rq', 'rk', 'rv', 'srg', 'su', 'ssg']
    dts = [BF16, F32, F32, BF16, BF16, BF16, ret_dtype, ret_dtype, ret_dtype, BF16, BF16, BF16]
    shapes = [(n * A_HEADS, A_DV) if nm in ('kf', 'vf') else (n, BRANCH_W) for nm in names]
    outs = pl.pallas_call(
        _inproj_kernel,
        grid=(nblk,),
        in_specs=[pl.BlockSpec((tm, D_MODEL), row),
                  pl.BlockSpec((1, D_MODEL), const),
                  pl.BlockSpec((D_MODEL, N_MIX), const, pipeline_mode=pl.Buffered(1)),
                  tspec, tspec, tspec, tspec,
                  pl.BlockSpec((1, LANES), const), pl.BlockSpec((1, LANES), const),
                  pl.BlockSpec((LANES, LANES), const)],
        out_specs=[hrow if nm in ('kf', 'vf') else sec for nm in names],
        out_shape=[jax.ShapeDtypeStruct(sh, dt) for sh, dt in zip(shapes, dts)],
        compiler_params=_cparams(("arbitrary",)),
        name="inproj",
    )(x, p['norm_gain'], p['w_mix'], tabs['cosa'], tabs['sina'], tabs['cosr'], tabs['sinr'],
      p['q_gain'], p['k_gain'], p['grp'])
    return dict(zip(names, outs))


def _diff_lambda(lq1, lk1, lq2, lk2, lam_init):
    return (jnp.exp(jnp.sum(lq1 * lk1, axis=-1, keepdims=True))
            - jnp.exp(jnp.sum(lq2 * lk2, axis=-1, keepdims=True)) + lam_init)


def _attn_finish(o, gain, sag, lam_init):
    ms = jnp.mean(o * o, axis=-1, keepdims=True)
    return o * lax.rsqrt(ms + EPS) * gain * (1.0 - lam_init) * sag


def _attn_prompt_kernel(qi_ref, ki_ref, q_ref, k_ref, v_ref, sag_ref, lq1_ref, lk1_ref, lq2_ref, lk2_ref,
                        gain_ref, o_ref, qs_scr, m_scr, l_scr, acc_scr, *, lam_init, tq):
    t = pl.program_id(1)
    qi = qi_ref[t]
    ki = ki_ref[t]
    lane = lax.broadcasted_iota(jnp.int32, (tq, LANES), 1)

    @pl.when(ki == 0)
    def _():
        for h in range(A_HEADS):
            qh = q_ref[:, h * LANES:(h + 1) * LANES]
            zero = jnp.zeros_like(qh)
            qs_scr[h, 0:tq, :] = jnp.where(lane < A_DH, qh, zero)
            qs_scr[h, tq:2 * tq, :] = jnp.where(lane >= A_DH, qh, zero)
        m_scr[...] = jnp.full(m_scr.shape, -jnp.inf, F32)
        l_scr[...] = jnp.zeros(l_scr.shape, F32)
        acc_scr[...] = jnp.zeros(acc_scr.shape, F32)

    def update(masked):
        for h in range(A_HEADS):
            kh = k_ref[:, h * LANES:(h + 1) * LANES]
            vh = v_ref[:, h * LANES:(h + 1) * LANES]
            s = lax.dot_general(qs_scr[h], kh, (((1,), (1,)), ((), ())), preferred_element_type=F32)
            if masked:
                r = lax.broadcasted_iota(jnp.int32, s.shape, 0)
                c = lax.broadcasted_iota(jnp.int32, s.shape, 1)
                r = jnp.where(r >= tq, r - tq, r)
                s = jnp.where(c <= r, s, -jnp.inf)
            m_prev = m_scr[h]
            m_new = jnp.maximum(m_prev, jnp.max(s, axis=-1, keepdims=True))
            alpha = jnp.exp2(m_prev - m_new)
            pexp = jnp.exp2(s - m_new)
            l_scr[h] = alpha * l_scr[h] + jnp.sum(pexp, axis=-1, keepdims=True)
            acc_scr[h] = alpha * acc_scr[h] + jnp.dot(pexp.astype(BF16), vh, preferred_element_type=F32)
            m_scr[h] = m_new

    @pl.when(ki < qi)
    def _():
        update(False)

    @pl.when(ki == qi)
    def _():
        update(True)
        lam = _diff_lambda(lq1_ref[...], lk1_ref[...], lq2_ref[...], lk2_ref[...], lam_init)
        for h in range(A_HEADS):
            acc = acc_scr[h]
            l = l_scr[h]
            o = acc[0:tq] / l[0:tq] - lam * (acc[tq:2 * tq] / l[tq:2 * tq])
            sl = slice(h * LANES, (h + 1) * LANES)
            o_ref[:, sl] = _attn_finish(o, gain_ref[:, sl], sag_ref[:, sl].astype(F32), lam_init).astype(BF16)


def _attn_prompt_shift_kernel(qi_ref, ki_ref, q_ref, k_ref, v_ref, sag_ref, lq1_ref, lk1_ref, lq2_ref, lk2_ref,
                              gain_ref, shift_ref, o_ref, qs_scr, acc_scr, *, lam_init, tq, rc):
    t = pl.program_id(1)
    qi = qi_ref[t]
    ki = ki_ref[t]
    lane = lax.broadcasted_iota(jnp.int32, (tq, LANES), 1)

    @pl.when(ki == 0)
    def _():
        for h in range(A_HEADS):
            qh = q_ref[:, h * LANES:(h + 1) * LANES]
            zero = jnp.zeros_like(qh)
            qs_scr[h, 0:tq, :] = jnp.where(lane < A_DH, qh, zero)
            qs_scr[h, tq:2 * tq, :] = jnp.where(lane >= A_DH, qh, zero)
        acc_scr[...] = jnp.zeros(acc_scr.shape, F32)

    shift = shift_ref[...]
    ones = jnp.ones((tq, LANES), BF16)

    def update(masked):
        for h in range(A_HEADS):
            sl = slice(h * LANES, (h + 1) * LANES)
            for r0 in range(0, 2 * tq, rc):
                kl = (r0 % tq) + rc if masked else tq
                vaug = jnp.concatenate([v_ref[0:kl, sl], ones[0:kl]], axis=1)
                s = lax.dot_general(qs_scr[h, r0:r0 + rc, :], k_ref[0:kl, sl], (((1,), (1,)), ((), ())),
                                    preferred_element_type=F32)
                if masked:
                    r = lax.broadcasted_iota(jnp.int32, s.shape, 0) + (r0 % tq)
                    c = lax.broadcasted_iota(jnp.int32, s.shape, 1)
                    s = jnp.where(c <= r, s, -jnp.inf)
                pexp = jnp.exp2(s - shift).astype(BF16)
                acc_scr[h, r0:r0 + rc, :] += jnp.dot(pexp, vaug, preferred_element_type=F32)

    @pl.when(ki < qi)
    def _():
        update(False)

    @pl.when(ki == qi)
    def _():
        update(True)
        lam = _diff_lambda(lq1_ref[...], lk1_ref[...], lq2_ref[...], lk2_ref[...], lam_init)
        for h in range(A_HEADS):
            a1 = acc_scr[h, 0:tq, :]
            a2 = acc_scr[h, tq:2 * tq, :]
            o = a1[:, 0:LANES] / a1[:, LANES:2 * LANES] - lam * (a2[:, 0:LANES] / a2[:, LANES:2 * LANES])
            sl = slice(h * LANES, (h + 1) * LANES)
            o_ref[:, sl] = _attn_finish(o, gain_ref[:, sl], sag_ref[:, sl].astype(F32), lam_init).astype(BF16)


ATTN_SHIFT_LIMIT = 48.0


def _attn_prompt(mix, p, nb, seq, lam_init, tq, rc, tq_running_max):
    n = nb * seq
    out_shape = jax.ShapeDtypeStruct((n, BRANCH_W), BF16)

    def call(kern, t, name, extra_specs, scratch, extra_args):
        nq = seq // t
        pairs = [(a, b) for a in range(nq) for b in range(a + 1)]
        qi = jnp.asarray(np.array([a for a, _ in pairs], np.int32))
        ki = jnp.asarray(np.array([b for _, b in pairs], np.int32))
        qmap = lambda b, s, qi, ki: (b * nq + qi[s], 0)
        kmap = lambda b, s, qi, ki: (b * nq + ki[s], 0)
        const = lambda b, s, qi, ki: (0, 0)
        blk = lambda m: pl.BlockSpec((t, BRANCH_W), m)
        vec = pl.BlockSpec((1, A_DH), const)
        return pl.pallas_call(
            kern,
            grid_spec=pltpu.PrefetchScalarGridSpec(
                num_scalar_prefetch=2, grid=(nb, len(pairs)),
                in_specs=[blk(qmap), blk(kmap), blk(kmap), blk(qmap), vec, vec, vec, vec,
                          pl.BlockSpec((1, BRANCH_W), const)] + [pl.BlockSpec(sh, const) for sh in extra_specs],
                out_specs=blk(qmap), scratch_shapes=scratch),
            out_shape=out_shape,
            compiler_params=_cparams(("arbitrary", "arbitrary")),
            name=name,
        )(qi, ki, mix['q'], mix['kb'], mix['vb'], mix['sag'], p['lam_q1'], p['lam_k1'], p['lam_q2'], p['lam_k2'],
          p['attn_gain'], *extra_args)

    def running_max(_):
        t = tq_running_max
        return call(functools.partial(_attn_prompt_kernel, lam_init=lam_init, tq=t), t, "attn_prompt", [],
                    [pltpu.VMEM((A_HEADS, 2 * t, LANES), BF16), pltpu.VMEM((A_HEADS, 2 * t, 1), F32),
                     pltpu.VMEM((A_HEADS, 2 * t, 1), F32), pltpu.VMEM((A_HEADS, 2 * t, LANES), F32)], [])

    def fixed_shift(shift):
        return call(functools.partial(_attn_prompt_shift_kernel, lam_init=lam_init, tq=tq, rc=rc), tq,
                    "attn_prompt_shift", [(1, 1)],
                    [pltpu.VMEM((A_HEADS, 2 * tq, LANES), BF16), pltpu.VMEM((A_HEADS, 2 * tq, 2 * LANES), F32)],
                    [shift])

    bound = (A_DH ** 0.5) * jnp.max(jnp.abs(p['q_gain'])) * jnp.max(jnp.abs(p['k_gain']))
    shift = (bound * LOG2E).reshape(1, 1).astype(F32)
    return lax.cond(2.0 * bound <= ATTN_SHIFT_LIMIT, fixed_shift, running_max, shift)


def _attn_sample_kernel(pt_ref, *refs, lam_init, pages):
    kc_refs = refs[0:pages]
    vc_refs = refs[pages:2 * pages]
    (qt_ref, kn_ref, vn_ref, sag_ref, lq1_ref, lk1_ref, lq2_ref, lk2_ref, gain_ref,
     o_ref, m_scr, l_scr, acc_scr) = refs[2 * pages:]
    s_id = pl.program_id(1)
    n_maps = 2 * A_HEADS
    qt = qt_ref[0]

    @pl.when(s_id == 0)
    def _():
        m_scr[...] = jnp.full(m_scr.shape, -jnp.inf, F32)
        l_scr[...] = jnp.zeros(l_scr.shape, F32)
        acc_scr[...] = jnp.zeros(acc_scr.shape, F32)

    kcat = jnp.concatenate([r[...].astype(BF16) for r in kc_refs], axis=0)
    vcat = jnp.concatenate([r[...].astype(BF16) for r in vc_refs], axis=0)
    s = lax.dot_general(qt.astype(BF16), kcat, (((1,), (1,)), ((), ())), preferred_element_type=F32)
    row = lax.broadcasted_iota(jnp.int32, s.shape, 0)
    col = lax.broadcasted_iota(jnp.int32, s.shape, 1)
    s = jnp.where(col % A_HEADS == row // 2, s, -jnp.inf)
    m_prev = m_scr[...]
    m_new = jnp.maximum(m_prev, jnp.max(s, axis=-1, keepdims=True))
    alpha = jnp.exp2(m_prev - m_new)
    pexp = jnp.exp2(s - m_new)
    l_scr[...] = alpha * l_scr[...] + jnp.sum(pexp, axis=-1, keepdims=True)
    acc_scr[...] = alpha * acc_scr[...] + jnp.dot(pexp.astype(BF16), vcat, preferred_element_type=F32)
    m_scr[...] = m_new

    @pl.when(s_id == pl.num_programs(1) - 1)
    def _():
        s_new = jnp.sum(qt * kn_ref[0], axis=-1, keepdims=True)
        m_prev = m_scr[...]
        m_fin = jnp.maximum(m_prev, s_new)
        alpha = jnp.exp2(m_prev - m_fin)
        p_new = jnp.exp2(s_new - m_fin)
        l = alpha * l_scr[...] + p_new
        acc = alpha * acc_scr[...] + p_new * vn_ref[0]
        lam = _diff_lambda(lq1_ref[...], lk1_ref[...], lq2_ref[...], lk2_ref[...], lam_init)
        r8 = lax.broadcasted_iota(jnp.int32, (n_maps, 1), 0)
        on = acc * (jnp.where(r8 % 2 == 0, 1.0, -lam) / l)
        for h in range(A_HEADS):
            sl = slice(h * LANES, (h + 1) * LANES)
            o = on[2 * h:2 * h + 1] + on[2 * h + 1:2 * h + 2]
            o_ref[0, :, sl] = _attn_finish(o, gain_ref[:, sl], sag_ref[0, :, sl].astype(F32),
                                           lam_init).astype(BF16)


def _attn_sample(mix, p, cache_k, cache_v, page_table, layer, lam_init, pages):
    nd, n_pages = page_table.shape
    steps = n_pages // pages
    pool = cache_k.shape[1]
    rows_pp = PAGE_SIZE * A_HEADS
    ck = cache_k.reshape(cache_k.shape[0], pool, rows_pp, A_DV)
    cv = cache_v.reshape(cache_v.shape[0], pool, rows_pp, A_DV)

    def page_spec(i):
        return pl.BlockSpec((None, None, rows_pp, A_DV),
                            lambda b, s, pt: (layer, pt[b * n_pages + s * pages + i], 0, 0))

    n_maps = 2 * A_HEADS
    tok = pl.BlockSpec((1, 1, BRANCH_W), lambda b, s, pt: (b, 0, 0))
    tok8 = pl.BlockSpec((1, n_maps, LANES), lambda b, s, pt: (b, 0, 0))
    const = lambda b, s, pt: (0, 0)
    vec = pl.BlockSpec((1, A_DH), const)
    half = (jnp.arange(LANES)[None, :] // A_DH == jnp.arange(2)[:, None]).astype(F32)
    qt = (mix['q'].astype(F32).reshape(nd, A_HEADS, 1, LANES) * half).reshape(nd, n_maps, LANES)
    rep = lambda a: jnp.repeat(a.reshape(nd, A_HEADS, LANES), 2, axis=1)
    out = pl.pallas_call(
        functools.partial(_attn_sample_kernel, lam_init=lam_init, pages=pages),
        grid_spec=pltpu.PrefetchScalarGridSpec(
            num_scalar_prefetch=1,
            grid=(nd, steps),
            in_specs=[page_spec(i) for i in range(pages)] * 2
                     + [tok8, tok8, tok8, tok, vec, vec, vec, vec, pl.BlockSpec((1, BRANCH_W), const)],
            out_specs=tok,
            scratch_shapes=[pltpu.VMEM((n_maps, 1), F32), pltpu.VMEM((n_maps, 1), F32),
                            pltpu.VMEM((n_maps, LANES), F32)]),
        out_shape=jax.ShapeDtypeStruct((nd, 1, BRANCH_W), BF16),
        compiler_params=_cparams(("arbitrary", "arbitrary")),
        name="attn_sample",
    )(page_table.reshape(-1), *([ck] * pages), *([cv] * pages),
      qt, rep(mix['kf']), rep(mix['vf']), mix['sag'].reshape(nd, 1, BRANCH_W),
      p['lam_q1'], p['lam_k1'], p['lam_q2'], p['lam_k2'], p['attn_gain'])
    return out.reshape(nd, BRANCH_W)


def _ret_finish(o, gain, srg):
    ms = jnp.mean(o * o, axis=-1, keepdims=True)
    return o * lax.rsqrt(ms + EPS) * gain * srg


def _ret_prompt_kernel(q_ref, k_ref, v_ref, srg_ref, gain_ref, o_ref, st_ref, s_scr, dm_scr, qd_scr, kd_scr, *, c):
    ci = pl.program_id(1)

    @pl.when(ci == 0)
    def _():
        s_scr[...] = jnp.zeros(s_scr.shape, F32)
        i = lax.broadcasted_iota(jnp.int32, (c, c), 0)
        jj = lax.broadcasted_iota(jnp.int32, (c, c), 1)
        d = (i - jj).astype(F32)
        r = lax.broadcasted_iota(jnp.int32, (c, LANES), 0).astype(F32)
        for h in range(R_HEADS):
            lg = RET_LOG_G[h]
            dm_scr[h] = jnp.where(i >= jj, jnp.exp(d * lg), 0.0)
            qd_scr[h] = jnp.exp((r + 1.0) * lg)
            kd_scr[h] = jnp.exp((c - 1.0 - r) * lg)

    for h in range(R_HEADS):
        sl = slice(h * LANES, (h + 1) * LANES)
        q = q_ref[:, sl]
        k = k_ref[:, sl]
        v = v_ref[:, sl]
        att = lax.dot_general(q, k, (((1,), (1,)), ((), ())), preferred_element_type=F32) * dm_scr[h]
        s_old = s_scr[h]
        o = (jnp.dot(att.astype(BF16), v, preferred_element_type=F32)
             + jnp.dot((q.astype(F32) * qd_scr[h]).astype(BF16), s_old.astype(BF16), preferred_element_type=F32))
        kd = (k.astype(F32) * kd_scr[h]).astype(BF16)
        s_new = math.exp(c * RET_LOG_G[h]) * s_old + lax.dot_general(
            kd, v, (((0,), (0,)), ((), ())), preferred_element_type=F32)
        s_scr[h] = s_new
        o_ref[:, sl] = _ret_finish(o, gain_ref[:, sl], srg_ref[:, sl].astype(F32)).astype(BF16)

    @pl.when(ci == pl.num_programs(1) - 1)
    def _():
        st_ref[0] = s_scr[...]


def _ret_prompt(mix, p, nb, seq, c):
    n = nb * seq
    nc = seq // c
    row = lambda b, i: (b * nc + i, 0)
    blk = pl.BlockSpec((c, BRANCH_W), row)
    return pl.pallas_call(
        functools.partial(_ret_prompt_kernel, c=c),
        grid=(nb, nc),
        in_specs=[blk, blk, blk, blk, pl.BlockSpec((1, BRANCH_W), lambda b, i: (0, 0))],
        out_specs=[blk, pl.BlockSpec((1, R_HEADS, R_DK, R_DV), lambda b, i: (b, 0, 0, 0))],
        out_shape=[jax.ShapeDtypeStruct((n, BRANCH_W), BF16),
                   jax.ShapeDtypeStruct((nb, R_HEADS, R_DK, R_DV), F32)],
        scratch_shapes=[pltpu.VMEM((R_HEADS, R_DK, R_DV), F32), pltpu.VMEM((R_HEADS, c, c), F32),
                        pltpu.VMEM((R_HEADS, c, LANES), F32), pltpu.VMEM((R_HEADS, c, LANES), F32)],
        compiler_params=_cparams(("arbitrary", "arbitrary")),
        name="ret_prompt",
    )(mix['rq'], mix['rk'], mix['rv'], mix['srg'], p['ret_gain'])


def _ret_sample_kernel(q_ref, k_ref, v_ref, srg_ref, gain_ref, s0_ref, o_ref, st_ref, *, nb):
    for b in range(nb):
        for h in range(R_HEADS):
            sl = slice(h * LANES, (h + 1) * LANES)
            g = math.exp(RET_LOG_G[h])
            q = q_ref[b, :, sl]
            k = k_ref[b, :, sl]
            v = v_ref[b, :, sl]
            s0 = s0_ref[b, h]
            qcol = jnp.transpose(jnp.broadcast_to(q, (SUBLANES, LANES)))[:, 0:1]
            kcol = jnp.transpose(jnp.broadcast_to(k, (SUBLANES, LANES)))[:, 0:1]
            o = jnp.sum(q * k, axis=-1, keepdims=True) * v + g * jnp.sum(qcol * s0, axis=0, keepdims=True)
            st_ref[b, h] = g * s0 + kcol * v
            o_ref[b, :, sl] = _ret_finish(o, gain_ref[:, sl], srg_ref[b, :, sl].astype(F32)).astype(BF16)


def _ret_sample(mix, p, s0, nb_blk):
    nd = s0.shape[0]
    r3 = lambda a: a.reshape(nd, 1, BRANCH_W)
    tok = pl.BlockSpec((nb_blk, 1, BRANCH_W), lambda i: (i, 0, 0))
    st = pl.BlockSpec((nb_blk, R_HEADS, R_DK, R_DV), lambda i: (i, 0, 0, 0))
    o, s1 = pl.pallas_call(
        functools.partial(_ret_sample_kernel, nb=nb_blk),
        grid=(nd // nb_blk,),
        in_specs=[tok, tok, tok, tok, pl.BlockSpec((1, BRANCH_W), lambda i: (0, 0)), st],
        out_specs=[tok, st],
        out_shape=[jax.ShapeDtypeStruct((nd, 1, BRANCH_W), BF16),
                   jax.ShapeDtypeStruct((nd, R_HEADS, R_DK, R_DV), F32)],
        compiler_params=_cparams(("arbitrary",)),
        name="ret_sample",
    )(r3(mix['rq']), r3(mix['rk']), r3(mix['rv']), r3(mix['srg']), p['ret_gain'], s0)
    return o.reshape(nd, BRANCH_W), s1


S5_LANE_BLK = 512
S5_HALF_W = BRANCH_W // 2
S5_HALF_CH = S_CH // 2


def _s5_cols(cb):
    per_half = S5_HALF_CH // S5_LANE_BLK
    base = (cb // per_half) * 2 * S5_HALF_CH + (cb % per_half) * S5_LANE_BLK
    return slice(base, base + S5_LANE_BLK), slice(base + S5_HALF_CH, base + S5_HALF_CH + S5_LANE_BLK)


def _cmul_add(ar, ai, hr, hi, br, bi):
    return ar * hr - ai * hi + br, ar * hi + ai * hr + bi


def _s5_prompt_kernel(u_ref, bmat_ref, cmat_ref, d_ref, ar_ref, ai_ref, atr_ref, ati_ref,
                      y_ref, hre_ref, him_ref, up_scr, bu_scr, yp_scr, car_scr, *, tm):
    it = pl.program_id(1)
    tp = tm // SUBLANES

    @pl.when(it == 0)
    def _():
        car_scr[...] = jnp.zeros(car_scr.shape, F32)

    u = u_ref[...].astype(F32)
    nlb = BRANCH_W // LANES
    for j in range(SUBLANES):
        for kb in range(nlb):
            up_scr[kb, pl.ds(j, tp, stride=SUBLANES), :] = u[j * tp:(j + 1) * tp, kb * LANES:(kb + 1) * LANES]
    up = jnp.concatenate([up_scr[kb] for kb in range(nlb)], axis=1)
    for b in range(2):
        bu_scr[:, b * 2 * S5_HALF_CH:(b + 1) * 2 * S5_HALF_CH] = jnp.dot(
            up[:, b * S5_HALF_W:(b + 1) * S5_HALF_W].astype(BF16), bmat_ref[b], preferred_element_type=F32)

    for cb in range(S_CH // S5_LANE_BLK):
        nat = slice(cb * S5_LANE_BLK, (cb + 1) * S5_LANE_BLK)
        re, im = _s5_cols(cb)
        ar = jnp.broadcast_to(ar_ref[:, nat], (SUBLANES, S5_LANE_BLK))
        ai = jnp.broadcast_to(ai_ref[:, nat], (SUBLANES, S5_LANE_BLK))

        def local_step(i, carry):
            hr, hi = carry
            r0 = pl.multiple_of(i * SUBLANES, SUBLANES)
            return _cmul_add(ar, ai, hr, hi, bu_scr[pl.ds(r0, SUBLANES), re], bu_scr[pl.ds(r0, SUBLANES), im])

        zero = jnp.zeros((SUBLANES, S5_LANE_BLK), F32)
        er, ei = lax.fori_loop(0, tp, local_step, (zero, zero))

        atr = atr_ref[:, nat]
        ati = ati_ref[:, nat]
        cr = car_scr[0:1, re]
        ci = car_scr[0:1, im]
        rows_r, rows_i = [cr], [ci]
        for j in range(1, SUBLANES):
            cr, ci = _cmul_add(atr, ati, cr, ci, er[j - 1:j], ei[j - 1:j])
            rows_r.append(cr)
            rows_i.append(ci)
        cr, ci = _cmul_add(atr, ati, cr, ci, er[SUBLANES - 1:SUBLANES], ei[SUBLANES - 1:SUBLANES])
        car_scr[0:1, re] = cr
        car_scr[0:1, im] = ci
        sr = jnp.concatenate(rows_r, axis=0)
        si = jnp.concatenate(rows_i, axis=0)

        def full_step(i, carry):
            hr, hi = carry
            r0 = pl.multiple_of(i * SUBLANES, SUBLANES)
            hr, hi = _cmul_add(ar, ai, hr, hi, bu_scr[pl.ds(r0, SUBLANES), re], bu_scr[pl.ds(r0, SUBLANES), im])
            bu_scr[pl.ds(r0, SUBLANES), re] = hr
            bu_scr[pl.ds(r0, SUBLANES), im] = hi
            return hr, hi

        lax.fori_loop(0, tp, full_step, (sr, si))

    yp = jnp.concatenate(
        [jnp.dot(bu_scr[:, b * 2 * S5_HALF_CH:(b + 1) * 2 * S5_HALF_CH].astype(BF16), cmat_ref[b],
                 preferred_element_type=F32) for b in range(2)], axis=1) + d_ref[...] * up
    for kb in range(nlb):
        yp_scr[kb] = yp[:, kb * LANES:(kb + 1) * LANES]
    for j in range(SUBLANES):
        for kb in range(nlb):
            y_ref[j * tp:(j + 1) * tp, kb * LANES:(kb + 1) * LANES] = _gelu_tanh(
                yp_scr[kb, pl.ds(j, tp, stride=SUBLANES), :]).astype(BF16)

    @pl.when(it == pl.num_programs(1) - 1)
    def _():
        hc = S5_HALF_CH
        hre_ref[0] = jnp.concatenate([car_scr[0:1, 0:hc], car_scr[0:1, 2 * hc:3 * hc]], axis=1)
        him_ref[0] = jnp.concatenate([car_scr[0:1, hc:2 * hc], car_scr[0:1, 3 * hc:4 * hc]], axis=1)


def _s5_prompt(mix, s5, nb, seq, tm):
    n = nb * seq
    nt = seq // tm
    row = lambda b, i: (b * nt + i, 0)
    const = lambda b, i: (0, 0)
    blk = pl.BlockSpec((tm, BRANCH_W), row)
    vec = pl.BlockSpec((1, S_CH), const)
    st = pl.BlockSpec((1, 1, S_CH), lambda b, i: (b, 0, 0))
    y, hre, him = pl.pallas_call(
        functools.partial(_s5_prompt_kernel, tm=tm),
        grid=(nb, nt),
        in_specs=[blk, pl.BlockSpec((2, S5_HALF_W, 2 * S5_HALF_CH), lambda b, i: (0, 0, 0)),
                  pl.BlockSpec((2, 2 * S5_HALF_CH, S5_HALF_W), lambda b, i: (0, 0, 0)),
                  pl.BlockSpec((1, BRANCH_W), const), vec, vec, vec, vec],
        out_specs=[blk, st, st],
        out_shape=[jax.ShapeDtypeStruct((n, BRANCH_W), BF16),
                   jax.ShapeDtypeStruct((nb, 1, S_CH), F32), jax.ShapeDtypeStruct((nb, 1, S_CH), F32)],
        scratch_shapes=[pltpu.VMEM((BRANCH_W // LANES, tm, LANES), F32), pltpu.VMEM((tm, 2 * S_CH), F32),
                        pltpu.VMEM((BRANCH_W // LANES, tm, LANES), F32), pltpu.VMEM((SUBLANES, 2 * S_CH), F32)],
        compiler_params=_cparams(("arbitrary", "arbitrary")),
        name="s5_prompt",
    )(mix['su'], s5['bmat'], s5['cmat'], s5['d'], s5['a_re'], s5['a_im'], s5['at_re'][tm // SUBLANES],
      s5['at_im'][tm // SUBLANES])
    return y, hre.reshape(nb, S_GROUPS, S_STATE), him.reshape(nb, S_GROUPS, S_STATE)


def _s5_sample_kernel(u_ref, bmat_ref, cmat_ref, d_ref, ar_ref, ai_ref, h0r_ref, h0i_ref, y_ref, hr_ref, hi_ref):
    u = u_ref[...]
    hc = S5_HALF_CH
    ys = []
    for b in range(2):
        ch = slice(b * hc, (b + 1) * hc)
        bu = jnp.dot(u[:, b * S5_HALF_W:(b + 1) * S5_HALF_W], bmat_ref[b], preferred_element_type=F32)
        hr, hi = _cmul_add(ar_ref[:, ch], ai_ref[:, ch], h0r_ref[:, ch], h0i_ref[:, ch], bu[:, 0:hc], bu[:, hc:2 * hc])
        hr_ref[:, ch] = hr
        hi_ref[:, ch] = hi
        h = jnp.concatenate([hr, hi], axis=-1).astype(BF16)
        ys.append(jnp.dot(h, cmat_ref[b], preferred_element_type=F32))
    y = jnp.concatenate(ys, axis=-1) + d_ref[...] * u.astype(F32)
    y_ref[...] = _gelu_tanh(y).astype(BF16)


def _s5_sample(mix, s5, h0_re, h0_im):
    nd = h0_re.shape[0]
    y, hr, hi = pl.pallas_call(
        _s5_sample_kernel,
        out_shape=[jax.ShapeDtypeStruct((nd, BRANCH_W), BF16),
                   jax.ShapeDtypeStruct((nd, S_CH), F32), jax.ShapeDtypeStruct((nd, S_CH), F32)],
        compiler_params=pltpu.CompilerParams(vmem_limit_bytes=VMEM_LIMIT),
        name="s5_sample",
    )(mix['su'], s5['bmat'], s5['cmat'], s5['d'], s5['a_re'], s5['a_im'],
      h0_re.reshape(nd, S_CH), h0_im.reshape(nd, S_CH))
    return y, hr.reshape(nd, S_GROUPS, S_STATE), hi.reshape(nd, S_GROUPS, S_STATE)


def _merge_kernel(x_ref, g_ref, wmg_ref, oa_ref, or_ref, ys_ref, ssg_ref, wba_ref, wbr_ref, wbs_ref,
                  wglu_ref, bglu_ref, wout_ref, out_ref):
    x = x_ref[...]
    ms = jnp.mean(x * x, axis=-1, keepdims=True)
    h = (x * lax.rsqrt(ms + EPS) * g_ref[...]).astype(BF16)
    gates = jax.nn.sigmoid(jnp.dot(h, wmg_ref[...], preferred_element_type=F32))
    ya = jnp.dot(oa_ref[...], wba_ref[...], preferred_element_type=F32)
    yr = jnp.dot(or_ref[...], wbr_ref[...], preferred_element_type=F32)
    glu = jnp.dot(ys_ref[...], wglu_ref[...], preferred_element_type=F32) + bglu_ref[...]
    s5 = glu[:, 0:BRANCH_W] * jax.nn.sigmoid(glu[:, BRANCH_W:2 * BRANCH_W]) * ssg_ref[...].astype(F32)
    y5 = jnp.dot(s5.astype(BF16), wbs_ref[...], preferred_element_type=F32)
    merged = (gates[:, 0:D_MODEL] * ya + gates[:, D_MODEL:2 * D_MODEL] * yr
              + gates[:, 2 * D_MODEL:3 * D_MODEL] * y5)
    out_ref[...] = x + jnp.dot(merged.astype(BF16), wout_ref[...], preferred_element_type=F32)


def _merge(x, o_attn, o_ret, y_s5, ssg, p, tm):
    n = x.shape[0]
    row = lambda i: (i, 0)
    const = lambda i: (0, 0)
    half = pl.BlockSpec((tm, BRANCH_W), row)
    full = lambda a: pl.BlockSpec(a.shape, const)
    ws = [p['w_br_attn'], p['w_br_ret'], p['w_br_s5'], p['w_glu'], p['b_glu'], p['w_out']]
    return pl.pallas_call(
        _merge_kernel,
        grid=(n // tm,),
        in_specs=[pl.BlockSpec((tm, D_MODEL), row), full(p['norm_gain']), full(p['w_mg']),
                  half, half, half, half] + [full(w) for w in ws],
        out_specs=pl.BlockSpec((tm, D_MODEL), row),
        out_shape=jax.ShapeDtypeStruct((n, D_MODEL), F32),
        compiler_params=_cparams(("arbitrary",)),
        name="merge",
    )(x, p['norm_gain'], p['w_mg'], o_attn, o_ret, y_s5, ssg, *ws)


def _rot_tables(pos, rows):
    pos = pos.astype(F32)[:, None]

    def tab(half, reps):
        inv = ROPE_THETA ** (-jnp.arange(half, dtype=F32) / half)
        ang = pos * inv[None, :]
        cos, sin = jnp.cos(ang), jnp.sin(ang)
        c = jnp.tile(jnp.concatenate([cos, cos], axis=-1), (1, reps))
        s = jnp.tile(jnp.concatenate([-sin, sin], axis=-1), (1, reps))
        if c.shape[0] != rows:
            c, s = jnp.broadcast_to(c, (rows, LANES)), jnp.broadcast_to(s, (rows, LANES))
        return c, s

    cosa, sina = tab(A_DH // 2, LANES // A_DH)
    cosr, sinr = tab(R_DK // 2, 1)
    return dict(cosa=cosa, sina=sina, cosr=cosr, sinr=sinr)


def _s5_params(lam_re, lam_im, log_dt, b_re, b_im, c_re, c_im, d_skip, powers):
    dt = jnp.exp(log_dt.astype(F32))[:, None]
    lre, lim = lam_re.astype(F32), lam_im.astype(F32)
    mag = jnp.exp(lre * dt)
    ab_re, ab_im = mag * jnp.cos(lim * dt), mag * jnp.sin(lim * dt)
    den = lre * lre + lim * lim
    nre = ab_re - 1.0
    cre = (nre * lre + ab_im * lim) / den
    cim = (ab_im * lre - nre * lim) / den
    b_re, b_im = b_re.astype(F32), b_im.astype(F32)
    bb_re = cre[..., None] * b_re - cim[..., None] * b_im
    bb_im = cre[..., None] * b_im + cim[..., None] * b_re
    eye = jnp.eye(S_GROUPS, dtype=F32)

    def in_mat(bb):
        return jnp.einsum('gpc,gh->gchp', bb, eye).reshape(BRANCH_W, S_CH)

    def out_mat(cc):
        return jnp.einsum('gcp,gh->gphc', cc.astype(F32), eye).reshape(S_CH, BRANCH_W)

    def halves(m, rows, cols):
        return [m[b * rows:(b + 1) * rows, b * cols:(b + 1) * cols] for b in range(2)]

    bmat = jnp.stack([jnp.concatenate([r, i], axis=1) for r, i in
                      zip(halves(in_mat(bb_re), S5_HALF_W, S5_HALF_CH),
                          halves(in_mat(bb_im), S5_HALF_W, S5_HALF_CH))]).astype(BF16)
    cmat = jnp.stack([jnp.concatenate([r, -i], axis=0) for r, i in
                      zip(halves(out_mat(c_re), S5_HALF_CH, S5_HALF_W),
                          halves(out_mat(c_im), S5_HALF_CH, S5_HALF_W))]).astype(BF16)
    a_re, a_im = ab_re.reshape(1, S_CH), ab_im.reshape(1, S_CH)
    at_re, at_im = {1: a_re}, {1: a_im}
    pr, pi, k = a_re, a_im, 1
    while k < max(powers):
        pr, pi, k = pr * pr - pi * pi, 2.0 * pr * pi, 2 * k
        at_re[k], at_im[k] = pr, pi
    return dict(bmat=bmat, cmat=cmat, d=d_skip.astype(F32).reshape(1, BRANCH_W), a_re=a_re, a_im=a_im,
                at_re=at_re, at_im=at_im)


TM_PROJ = 512
TQ_ATTN = 1024
RC_ATTN = 256
TQ_ATTN_RUNNING_MAX = 512
RET_CHUNK = 256
TM_S5 = 512
TM_MERGE = 512
SAMPLE_PAGES = 16
RET_SAMPLE_BLK = 8


def kernel(x_prompt, x_sample, cache_k, cache_v, state_ret, state_s5_re, state_s5_im, page_table, norm_gain, w_in, q_norm_gain, k_norm_gain, lam_q1, lam_k1, lam_q2, lam_k2, attn_out_gain, ret_out_gain, s5_lam_re, s5_lam_im, s5_log_dt, s5_b_re, s5_b_im, s5_c_re, s5_c_im, s5_d, w_glu, b_glu, w_br_attn, w_br_ret, w_br_s5, w_out):
    depth = w_in.shape[0]
    nb, seq, _ = x_prompt.shape
    nd, dseq, _ = x_sample.shape
    assert dseq == 1, "the sample group is a single-token step"
    n_pages = page_table.shape[1]
    past_len = n_pages * PAGE_SIZE
    tm_proj, tq, c_ret = min(TM_PROJ, seq), min(TQ_ATTN, seq), min(RET_CHUNK, seq)
    tm_s5, tm_merge = min(TM_S5, seq), min(TM_MERGE, seq)
    pages = min(SAMPLE_PAGES, n_pages)
    assert seq % tm_proj == 0 and seq % tq == 0 and seq % min(TQ_ATTN_RUNNING_MAX, seq) == 0 and seq % c_ret == 0 and seq % tm_s5 == 0 and seq % tm_merge == 0
    assert n_pages % pages == 0 and nd % RET_SAMPLE_BLK == 0

    tabs_p = _rot_tables(jnp.arange(seq), seq)
    tabs_s = _rot_tables(past_len + jnp.arange(1), nd)
    grp = (jnp.arange(LANES)[:, None] // A_DH == jnp.arange(LANES)[None, :] // A_DH).astype(BF16)

    yp = x_prompt.reshape(nb * seq, D_MODEL)
    ys = x_sample.reshape(nd, D_MODEL)
    outs = {k: [] for k in ('kp', 'vp', 'rp', 'spr', 'spi', 'ks', 'vs', 'rs', 'ssr', 'ssi')}
    for l in range(depth):
        lam_init = 0.8 - 0.6 * math.exp(-0.3 * l)
        w_l = w_in[l].astype(BF16)
        p = dict(
            norm_gain=norm_gain[l].reshape(1, D_MODEL), w_mix=w_l[:, :N_MIX], w_mg=w_l[:, N_MIX:], grp=grp,
            q_gain=jnp.tile(q_norm_gain[l], LANES // A_DH).reshape(1, LANES),
            k_gain=jnp.tile(k_norm_gain[l], LANES // A_DH).reshape(1, LANES),
            lam_q1=lam_q1[l].reshape(1, A_DH), lam_k1=lam_k1[l].reshape(1, A_DH),
            lam_q2=lam_q2[l].reshape(1, A_DH), lam_k2=lam_k2[l].reshape(1, A_DH),
            attn_gain=attn_out_gain[l].reshape(1, BRANCH_W), ret_gain=ret_out_gain[l].reshape(1, BRANCH_W),
            w_glu=w_glu[l].astype(BF16), b_glu=b_glu[l].reshape(1, 2 * BRANCH_W),
            w_br_attn=w_br_attn[l].astype(BF16), w_br_ret=w_br_ret[l].astype(BF16),
            w_br_s5=w_br_s5[l].astype(BF16), w_out=w_out[l].astype(BF16))
        s5 = _s5_params(s5_lam_re[l], s5_lam_im[l], s5_log_dt[l], s5_b_re[l], s5_b_im[l], s5_c_re[l], s5_c_im[l],
                        s5_d[l], (tm_s5 // SUBLANES,))

        mix = _inproj(yp, p, tabs_p, tm_proj, seq, BF16)
        o_attn = _attn_prompt(mix, p, nb, seq, lam_init, tq, min(RC_ATTN, tq), min(TQ_ATTN_RUNNING_MAX, seq))
        o_ret, ret_state = _ret_prompt(mix, p, nb, seq, c_ret)
        y_s5, s_re, s_im = _s5_prompt(mix, s5, nb, seq, tm_s5)
        yp = _merge(yp, o_attn, o_ret, y_s5, mix['ssg'], p, tm_merge)
        outs['kp'].append(mix['kf'].reshape(nb, seq, A_HEADS, A_DV))
        outs['vp'].append(mix['vf'].reshape(nb, seq, A_HEADS, A_DV))
        outs['rp'].append(ret_state)
        outs['spr'].append(s_re)
        outs['spi'].append(s_im)

        mix = _inproj(ys, p, tabs_s, nd, nd, F32)
        o_attn = _attn_sample(mix, p, cache_k, cache_v, page_table, l, lam_init, pages)
        o_ret, ret_state = _ret_sample(mix, p, state_ret[l], RET_SAMPLE_BLK)
        y_s5, s_re, s_im = _s5_sample(mix, s5, state_s5_re[l], state_s5_im[l])
        ys = _merge(ys, o_attn, o_ret, y_s5, mix['ssg'], p, nd)
        outs['ks'].append(mix['kf'].reshape(nd, 1, A_HEADS, A_DV))
        outs['vs'].append(mix['vf'].reshape(nd, 1, A_HEADS, A_DV))
        outs['rs'].append(ret_state)
        outs['ssr'].append(s_re)
        outs['ssi'].append(s_im)

    st = lambda k: jnp.stack(outs[k])
    return (yp.reshape(nb, seq, D_MODEL), ys.reshape(nd, 1, D_MODEL),
            st('kp'), st('vp'), st('rp'), st('spr'), st('spi'),
            st('ks'), st('vs'), st('rs'), st('ssr'), st('ssi'))
```

```python
import functools
import math

import jax
import jax.numpy as jnp
import numpy as np
from jax import lax
from jax.experimental import pallas as pl
from jax.experimental.pallas import tpu as pltpu

F32 = jnp.float32
BF16 = jnp.bfloat16

LANES = 128
SUBLANES = 8
VMEM_LIMIT = 56 * 1024 * 1024

D_MODEL = 1024
BRANCH_W = D_MODEL // 2
A_HEADS = 4
A_DH = BRANCH_W // (2 * A_HEADS)
A_DV = 2 * A_DH
R_HEADS = 4
R_DK = BRANCH_W // R_HEADS
R_DV = BRANCH_W // R_HEADS
S_GROUP = 16
S_GROUPS = BRANCH_W // S_GROUP
S_STATE = 64
S_CH = S_GROUPS * S_STATE
N_BRANCH = 3
N_MIX = 10 * BRANCH_W
PAGE_SIZE = 128
ROPE_THETA = 10000.0
EPS = 1e-6
LOG2E = math.log2(math.e)
RET_LOG_G = tuple(math.log1p(-2.0 ** (-5.0 - h)) for h in range(R_HEADS))

SEC_AQ, SEC_AK, SEC_AV, SEC_AG, SEC_RQ, SEC_RK, SEC_RV, SEC_RG, SEC_SU, SEC_SG = range(10)


def _cparams(sem):
    return pltpu.CompilerParams(dimension_semantics=sem, vmem_limit_bytes=VMEM_LIMIT)


def _silu(x):
    return x * jax.nn.sigmoid(x)


def _gelu_tanh(x):
    return 0.5 * x * (1.0 + jnp.tanh(math.sqrt(2.0 / math.pi) * (x + 0.044715 * x * x * x)))


N_INPROJ_IN = 10


def _inproj_kernel(*refs):
    x_ref, g_ref, w_ref, cosa_ref, sina_ref, cosr_ref, sinr_ref, qg_ref, kg_ref, grp_ref = refs[:N_INPROJ_IN]
    (q_ref, kf_ref, vf_ref, kb_ref, vb_ref, sag_ref, rq_ref, rk_ref, rv_ref, srg_ref,
     su_ref, ssg_ref) = refs[-12:]
    x = x_ref[...]
    tm = x.shape[0]
    ms = jnp.mean(x * x, axis=-1, keepdims=True)
    hn = (x * lax.rsqrt(ms + EPS) * g_ref[...]).astype(BF16)
    lane = lax.broadcasted_iota(jnp.int32, (tm, LANES), 1)

    def section(s):
        return jnp.dot(hn, w_ref[:, s * BRANCH_W:(s + 1) * BRANCH_W], preferred_element_type=F32)

    def head(t, h):
        return t[:, h * LANES:(h + 1) * LANES]

    def qk_norm_rot(zh, gain):
        ss = jnp.dot((zh * zh).astype(BF16), grp_ref[...], preferred_element_type=F32)
        y = zh * lax.rsqrt(ss * (1.0 / A_DH) + EPS) * gain
        half = A_DH // 2
        swapped = jnp.where((lane & half) == 0, pltpu.roll(y, LANES - half, 1), pltpu.roll(y, half, 1))
        return y * cosa_ref[...] + swapped * sina_ref[...]

    def ret_rot(zh):
        return zh * cosr_ref[...] + pltpu.roll(zh, R_DK // 2, 1) * sinr_ref[...]

    def head_rows(h):
        return pl.ds(h, tm, stride=A_HEADS)

    z = section(SEC_AQ)
    for h in range(A_HEADS):
        y = qk_norm_rot(head(z, h), qg_ref[...]) * (A_DH ** -0.5 * LOG2E)
        q_ref[:, h * LANES:(h + 1) * LANES] = y.astype(BF16)

    z = section(SEC_AK)
    for h in range(A_HEADS):
        y = qk_norm_rot(head(z, h), kg_ref[...])
        kf_ref[head_rows(h), :] = y
        kb_ref[:, h * LANES:(h + 1) * LANES] = y.astype(BF16)

    z = section(SEC_AV)
    for h in range(A_HEADS):
        vf_ref[head_rows(h), :] = head(z, h)
    vb_ref[...] = z.astype(BF16)

    sag_ref[...] = _silu(section(SEC_AG)).astype(BF16)

    z = section(SEC_RQ)
    for h in range(R_HEADS):
        rq_ref[:, h * LANES:(h + 1) * LANES] = ret_rot(head(z, h)).astype(rq_ref.dtype)

    z = section(SEC_RK)
    for h in range(R_HEADS):
        rk_ref[:, h * LANES:(h + 1) * LANES] = (ret_rot(head(z, h)) * (R_DK ** -0.5)).astype(rk_ref.dtype)

    rv_ref[...] = section(SEC_RV).astype(rv_ref.dtype)
    srg_ref[...] = _silu(section(SEC_RG)).astype(BF16)
    su_ref[...] = section(SEC_SU).astype(BF16)
    ssg_ref[...] = _silu(section(SEC_SG)).astype(BF16)


def _inproj(x, p, w_all, tabs, tm, rows_per_seq, ret_dtype, layer, depth, kv_prev):
    n = x.shape[0]
    nblk = n // tm
    tab_blocks = rows_per_seq // tm if rows_per_seq >= tm else 1
    row = lambda i: (i, 0)
    tab = lambda i: (i % tab_blocks, 0)
    const = lambda i: (0, 0)
    sec = pl.BlockSpec((tm, BRANCH_W), row)
    hrow = pl.BlockSpec((tm * A_HEADS, A_DV), lambda i: (layer * nblk + i, 0))
    tspec = pl.BlockSpec((tm, LANES), tab)
    names = ['q', 'kf', 'vf', 'kb', 'vb', 'sag', 'rq', 'rk', 'rv', 'srg', 'su', 'ssg']
    dts = [BF16, F32, F32, BF16, BF16, BF16, ret_dtype, ret_dtype, ret_dtype, BF16, BF16, BF16]
    shapes = [(depth * n * A_HEADS, A_DV) if nm in ('kf', 'vf') else (n, BRANCH_W) for nm in names]
    in_specs = [pl.BlockSpec((tm, D_MODEL), row),
                pl.BlockSpec((1, D_MODEL), const),
                pl.BlockSpec((None, D_MODEL, N_MIX), lambda i: (layer, 0, 0), pipeline_mode=pl.Buffered(1)),
                tspec, tspec, tspec, tspec,
                pl.BlockSpec((1, LANES), const), pl.BlockSpec((1, LANES), const),
                pl.BlockSpec((LANES, LANES), const)]
    args = [x, p['norm_gain'], w_all, tabs['cosa'], tabs['sina'], tabs['cosr'], tabs['sinr'],
            p['q_gain'], p['k_gain'], p['grp']]
    assert len(args) == N_INPROJ_IN
    aliases = {}
    if kv_prev is not None:
        in_specs += [pl.BlockSpec(memory_space=pl.ANY)] * 2
        args += list(kv_prev)
        aliases = {N_INPROJ_IN: names.index('kf'), N_INPROJ_IN + 1: names.index('vf')}
    outs = pl.pallas_call(
        _inproj_kernel,
        grid=(nblk,),
        in_specs=in_specs,
        out_specs=[hrow if nm in ('kf', 'vf') else sec for nm in names],
        out_shape=[jax.ShapeDtypeStruct(sh, dt) for sh, dt in zip(shapes, dts)],
        input_output_aliases=aliases,
        compiler_params=_cparams(("arbitrary",)),
        name="inproj",
    )(*args)
    return dict(zip(names, outs))


def _diff_lambda(lq1, lk1, lq2, lk2, lam_init):
    return (jnp.exp(jnp.sum(lq1 * lk1, axis=-1, keepdims=True))
            - jnp.exp(jnp.sum(lq2 * lk2, axis=-1, keepdims=True)) + lam_init)


def _attn_finish(o, gain, sag, lam_init):
    ms = jnp.mean(o * o, axis=-1, keepdims=True)
    return o * lax.rsqrt(ms + EPS) * gain * (1.0 - lam_init) * sag


def _attn_prompt_kernel(qi_ref, ki_ref, q_ref, k_ref, v_ref, sag_ref, lq1_ref, lk1_ref, lq2_ref, lk2_ref,
                        gain_ref, o_ref, qs_scr, m_scr, l_scr, acc_scr, *, lam_init, tq):
    t = pl.program_id(1)
    qi = qi_ref[t]
    ki = ki_ref[t]
    lane = lax.broadcasted_iota(jnp.int32, (tq, LANES), 1)

    @pl.when(ki == 0)
    def _():
        for h in range(A_HEADS):
            qh = q_ref[:, h * LANES:(h + 1) * LANES]
            zero = jnp.zeros_like(qh)
            qs_scr[h, 0:tq, :] = jnp.where(lane < A_DH, qh, zero)
            qs_scr[h, tq:2 * tq, :] = jnp.where(lane >= A_DH, qh, zero)
        m_scr[...] = jnp.full(m_scr.shape, -jnp.inf, F32)
        l_scr[...] = jnp.zeros(l_scr.shape, F32)
        acc_scr[...] = jnp.zeros(acc_scr.shape, F32)

    def update(masked):
        for h in range(A_HEADS):
            kh = k_ref[:, h * LANES:(h + 1) * LANES]
            vh = v_ref[:, h * LANES:(h + 1) * LANES]
            s = lax.dot_general(qs_scr[h], kh, (((1,), (1,)), ((), ())), preferred_element_type=F32)
            if masked:
                r = lax.broadcasted_iota(jnp.int32, s.shape, 0)
                c = lax.broadcasted_iota(jnp.int32, s.shape, 1)
                r = jnp.where(r >= tq, r - tq, r)
                s = jnp.where(c <= r, s, -jnp.inf)
            m_prev = m_scr[h]
            m_new = jnp.maximum(m_prev, jnp.max(s, axis=-1, keepdims=True))
            alpha = jnp.exp2(m_prev - m_new)
            pexp = jnp.exp2(s - m_new)
            l_scr[h] = alpha * l_scr[h] + jnp.sum(pexp, axis=-1, keepdims=True)
            acc_scr[h] = alpha * acc_scr[h] + jnp.dot(pexp.astype(BF16), vh, preferred_element_type=F32)
            m_scr[h] = m_new

    @pl.when(ki < qi)
    def _():
        update(False)

    @pl.when(ki == qi)
    def _():
        update(True)
        lam = _diff_lambda(lq1_ref[...], lk1_ref[...], lq2_ref[...], lk2_ref[...], lam_init)
        for h in range(A_HEADS):
            acc = acc_scr[h]
            l = l_scr[h]
            o = acc[0:tq] / l[0:tq] - lam * (acc[tq:2 * tq] / l[tq:2 * tq])
            sl = slice(h * LANES, (h + 1) * LANES)
            o_ref[:, sl] = _attn_finish(o, gain_ref[:, sl], sag_ref[:, sl].astype(F32), lam_init).astype(BF16)


def _attn_prompt_shift_kernel(qi_ref, ki_ref, q_ref, k_ref, v_ref, sag_ref, lq1_ref, lk1_ref, lq2_ref, lk2_ref,
                              gain_ref, shift_ref, o_ref, qs_scr, acc_scr, *, lam_init, tq, rc):
    t = pl.program_id(1)
    qi = qi_ref[t]
    ki = ki_ref[t]
    lane = lax.broadcasted_iota(jnp.int32, (tq, LANES), 1)

    @pl.when(ki == 0)
    def _():
        for h in range(A_HEADS):
            qh = q_ref[:, h * LANES:(h + 1) * LANES]
            zero = jnp.zeros_like(qh)
            qs_scr[h, 0:tq, :] = jnp.where(lane < A_DH, qh, zero)
            qs_scr[h, tq:2 * tq, :] = jnp.where(lane >= A_DH, qh, zero)
        acc_scr[...] = jnp.zeros(acc_scr.shape, F32)

    shift = shift_ref[...]
    ones = jnp.ones((tq, LANES), BF16)

    def update(masked):
        for h in range(A_HEADS):
            sl = slice(h * LANES, (h + 1) * LANES)
            for r0 in range(0, 2 * tq, rc):
                kl = (r0 % tq) + rc if masked else tq
                vaug = jnp.concatenate([v_ref[0:kl, sl], ones[0:kl]], axis=1)
                s = lax.dot_general(qs_scr[h, r0:r0 + rc, :], k_ref[0:kl, sl], (((1,), (1,)), ((), ())),
                                    preferred_element_type=F32)
                if masked:
                    r = lax.broadcasted_iota(jnp.int32, s.shape, 0) + (r0 % tq)
                    c = lax.broadcasted_iota(jnp.int32, s.shape, 1)
                    s = jnp.where(c <= r, s, -jnp.inf)
                pexp = jnp.exp2(s - shift).astype(BF16)
                acc_scr[h, r0:r0 + rc, :] += jnp.dot(pexp, vaug, preferred_element_type=F32)

    @pl.when(ki < qi)
    def _():
        update(False)

    @pl.when(ki == qi)
    def _():
        update(True)
        lam = _diff_lambda(lq1_ref[...], lk1_ref[...], lq2_ref[...], lk2_ref[...], lam_init)
        for h in range(A_HEADS):
            a1 = acc_scr[h, 0:tq, :]
            a2 = acc_scr[h, tq:2 * tq, :]
            o = a1[:, 0:LANES] / a1[:, LANES:2 * LANES] - lam * (a2[:, 0:LANES] / a2[:, LANES:2 * LANES])
            sl = slice(h * LANES, (h + 1) * LANES)
            o_ref[:, sl] = _attn_finish(o, gain_ref[:, sl], sag_ref[:, sl].astype(F32), lam_init).astype(BF16)


ATTN_SHIFT_LIMIT = 48.0


def _attn_prompt(mix, p, nb, seq, lam_init, tq, rc, tq_running_max):
    n = nb * seq
    out_shape = jax.ShapeDtypeStruct((n, BRANCH_W), BF16)

    def call(kern, t, name, extra_specs, scratch, extra_args):
        nq = seq // t
        pairs = [(a, b) for a in range(nq) for b in range(a + 1)]
        qi = jnp.asarray(np.array([a for a, _ in pairs], np.int32))
        ki = jnp.asarray(np.array([b for _, b in pairs], np.int32))
        qmap = lambda b, s, qi, ki: (b * nq + qi[s], 0)
        kmap = lambda b, s, qi, ki: (b * nq + ki[s], 0)
        const = lambda b, s, qi, ki: (0, 0)
        blk = lambda m: pl.BlockSpec((t, BRANCH_W), m)
        vec = pl.BlockSpec((1, A_DH), const)
        return pl.pallas_call(
            kern,
            grid_spec=pltpu.PrefetchScalarGridSpec(
                num_scalar_prefetch=2, grid=(nb, len(pairs)),
                in_specs=[blk(qmap), blk(kmap), blk(kmap), blk(qmap), vec, vec, vec, vec,
                          pl.BlockSpec((1, BRANCH_W), const)] + [pl.BlockSpec(sh, const) for sh in extra_specs],
                out_specs=blk(qmap), scratch_shapes=scratch),
            out_shape=out_shape,
            compiler_params=_cparams(("arbitrary", "arbitrary")),
            name=name,
        )(qi, ki, mix['q'], mix['kb'], mix['vb'], mix['sag'], p['lam_q1'], p['lam_k1'], p['lam_q2'], p['lam_k2'],
          p['attn_gain'], *extra_args)

    def running_max(_):
        t = tq_running_max
        return call(functools.partial(_attn_prompt_kernel, lam_init=lam_init, tq=t), t, "attn_prompt", [],
                    [pltpu.VMEM((A_HEADS, 2 * t, LANES), BF16), pltpu.VMEM((A_HEADS, 2 * t, 1), F32),
                     pltpu.VMEM((A_HEADS, 2 * t, 1), F32), pltpu.VMEM((A_HEADS, 2 * t, LANES), F32)], [])

    def fixed_shift(shift):
        return call(functools.partial(_attn_prompt_shift_kernel, lam_init=lam_init, tq=tq, rc=rc), tq,
                    "attn_prompt_shift", [(1, 1)],
                    [pltpu.VMEM((A_HEADS, 2 * tq, LANES), BF16), pltpu.VMEM((A_HEADS, 2 * tq, 2 * LANES), F32)],
                    [shift])

    bound = (A_DH ** 0.5) * jnp.max(jnp.abs(p['q_gain'])) * jnp.max(jnp.abs(p['k_gain']))
    shift = (bound * LOG2E).reshape(1, 1).astype(F32)
    return lax.cond(2.0 * bound <= ATTN_SHIFT_LIMIT, fixed_shift, running_max, shift)


def _attn_sample_kernel(pt_ref, *refs, lam_init, pages):
    kc_refs = refs[0:pages]
    vc_refs = refs[pages:2 * pages]
    (qt_ref, kn_ref, vn_ref, sag_ref, lq1_ref, lk1_ref, lq2_ref, lk2_ref, gain_ref,
     o_ref, m_scr, l_scr, acc_scr) = refs[2 * pages:]
    s_id = pl.program_id(1)
    n_maps = 2 * A_HEADS
    qt = qt_ref[0]

    @pl.when(s_id == 0)
    def _():
        m_scr[...] = jnp.full(m_scr.shape, -jnp.inf, F32)
        l_scr[...] = jnp.zeros(l_scr.shape, F32)
        acc_scr[...] = jnp.zeros(acc_scr.shape, F32)

    kcat = jnp.concatenate([r[...].astype(BF16) for r in kc_refs], axis=0)
    vcat = jnp.concatenate([r[...].astype(BF16) for r in vc_refs], axis=0)
    s = lax.dot_general(qt.astype(BF16), kcat, (((1,), (1,)), ((), ())), preferred_element_type=F32)
    row = lax.broadcasted_iota(jnp.int32, s.shape, 0)
    col = lax.broadcasted_iota(jnp.int32, s.shape, 1)
    s = jnp.where(col % A_HEADS == row // 2, s, -jnp.inf)
    m_prev = m_scr[...]
    m_new = jnp.maximum(m_prev, jnp.max(s, axis=-1, keepdims=True))
    alpha = jnp.exp2(m_prev - m_new)
    pexp = jnp.exp2(s - m_new)
    l_scr[...] = alpha * l_scr[...] + jnp.sum(pexp, axis=-1, keepdims=True)
    acc_scr[...] = alpha * acc_scr[...] + jnp.dot(pexp.astype(BF16), vcat, preferred_element_type=F32)
    m_scr[...] = m_new

    @pl.when(s_id == pl.num_programs(1) - 1)
    def _():
        s_new = jnp.sum(qt * kn_ref[0], axis=-1, keepdims=True)
        m_prev = m_scr[...]
        m_fin = jnp.maximum(m_prev, s_new)
        alpha = jnp.exp2(m_prev - m_fin)
        p_new = jnp.exp2(s_new - m_fin)
        l = alpha * l_scr[...] + p_new
        acc = alpha * acc_scr[...] + p_new * vn_ref[0]
        lam = _diff_lambda(lq1_ref[...], lk1_ref[...], lq2_ref[...], lk2_ref[...], lam_init)
        r8 = lax.broadcasted_iota(jnp.int32, (n_maps, 1), 0)
        on = acc * (jnp.where(r8 % 2 == 0, 1.0, -lam) / l)
        for h in range(A_HEADS):
            sl = slice(h * LANES, (h + 1) * LANES)
            o = on[2 * h:2 * h + 1] + on[2 * h + 1:2 * h + 2]
            o_ref[0, :, sl] = _attn_finish(o, gain_ref[:, sl], sag_ref[0, :, sl].astype(F32),
                                           lam_init).astype(BF16)


def _attn_sample(mix, p, cache_k, cache_v, page_table, layer, lam_init, pages):
    nd, n_pages = page_table.shape
    new_rows = slice(layer * nd * A_HEADS, (layer + 1) * nd * A_HEADS)
    steps = n_pages // pages
    pool = cache_k.shape[1]
    rows_pp = PAGE_SIZE * A_HEADS
    ck = cache_k.reshape(cache_k.shape[0], pool, rows_pp, A_DV)
    cv = cache_v.reshape(cache_v.shape[0], pool, rows_pp, A_DV)

    def page_spec(i):
        return pl.BlockSpec((None, None, rows_pp, A_DV),
                            lambda b, s, pt: (layer, pt[b * n_pages + s * pages + i], 0, 0))

    n_maps = 2 * A_HEADS
    tok = pl.BlockSpec((1, 1, BRANCH_W), lambda b, s, pt: (b, 0, 0))
    tok8 = pl.BlockSpec((1, n_maps, LANES), lambda b, s, pt: (b, 0, 0))
    const = lambda b, s, pt: (0, 0)
    vec = pl.BlockSpec((1, A_DH), const)
    half = (jnp.arange(LANES)[None, :] // A_DH == jnp.arange(2)[:, None]).astype(F32)
    qt = (mix['q'].astype(F32).reshape(nd, A_HEADS, 1, LANES) * half).reshape(nd, n_maps, LANES)
    rep = lambda a: jnp.repeat(a.reshape(nd, A_HEADS, LANES), 2, axis=1)
    out = pl.pallas_call(
        functools.partial(_attn_sample_kernel, lam_init=lam_init, pages=pages),
        grid_spec=pltpu.PrefetchScalarGridSpec(
            num_scalar_prefetch=1,
            grid=(nd, steps),
            in_specs=[page_spec(i) for i in range(pages)] * 2
                     + [tok8, tok8, tok8, tok, vec, vec, vec, vec, pl.BlockSpec((1, BRANCH_W), const)],
            out_specs=tok,
            scratch_shapes=[pltpu.VMEM((n_maps, 1), F32), pltpu.VMEM((n_maps, 1), F32),
                            pltpu.VMEM((n_maps, LANES), F32)]),
        out_shape=jax.ShapeDtypeStruct((nd, 1, BRANCH_W), BF16),
        compiler_params=_cparams(("arbitrary", "arbitrary")),
        name="attn_sample",
    )(page_table.reshape(-1), *([ck] * pages), *([cv] * pages),
      qt, rep(mix['kf'][new_rows]), rep(mix['vf'][new_rows]), mix['sag'].reshape(nd, 1, BRANCH_W),
      p['lam_q1'], p['lam_k1'], p['lam_q2'], p['lam_k2'], p['attn_gain'])
    return out.reshape(nd, BRANCH_W)


def _ret_finish(o, gain, srg):
    ms = jnp.mean(o * o, axis=-1, keepdims=True)
    return o * lax.rsqrt(ms + EPS) * gain * srg


def _ret_prompt_kernel(q_ref, k_ref, v_ref, srg_ref, gain_ref, o_ref, st_ref, s_scr, dm_scr, qd_scr, kd_scr, *, c):
    ci = pl.program_id(1)

    @pl.when(ci == 0)
    def _():
        s_scr[...] = jnp.zeros(s_scr.shape, F32)
        i = lax.broadcasted_iota(jnp.int32, (c, c), 0)
        jj = lax.broadcasted_iota(jnp.int32, (c, c), 1)
        d = (i - jj).astype(F32)
        r = lax.broadcasted_iota(jnp.int32, (c, LANES), 0).astype(F32)
        for h in range(R_HEADS):
            lg = RET_LOG_G[h]
            dm_scr[h] = jnp.where(i >= jj, jnp.exp(d * lg), 0.0)
            qd_scr[h] = jnp.exp((r + 1.0) * lg)
            kd_scr[h] = jnp.exp((c - 1.0 - r) * lg)

    for h in range(R_HEADS):
        sl = slice(h * LANES, (h + 1) * LANES)
        q = q_ref[:, sl]
        k = k_ref[:, sl]
        v = v_ref[:, sl]
        att = lax.dot_general(q, k, (((1,), (1,)), ((), ())), preferred_element_type=F32) * dm_scr[h]
        s_old = s_scr[h]
        o = (jnp.dot(att.astype(BF16), v, preferred_element_type=F32)
             + jnp.dot((q.astype(F32) * qd_scr[h]).astype(BF16), s_old.astype(BF16), preferred_element_type=F32))
        kd = (k.astype(F32) * kd_scr[h]).astype(BF16)
        s_new = math.exp(c * RET_LOG_G[h]) * s_old + lax.dot_general(
            kd, v, (((0,), (0,)), ((), ())), preferred_element_type=F32)
        s_scr[h] = s_new
        o_ref[:, sl] = _ret_finish(o, gain_ref[:, sl], srg_ref[:, sl].astype(F32)).astype(BF16)

    @pl.when(ci == pl.num_programs(1) - 1)
    def _():
        st_ref[0] = s_scr[...]


def _ret_prompt(mix, p, nb, seq, c):
    n = nb * seq
    nc = seq // c
    row = lambda b, i: (b * nc + i, 0)
    blk = pl.BlockSpec((c, BRANCH_W), row)
    return pl.pallas_call(
        functools.partial(_ret_prompt_kernel, c=c),
        grid=(nb, nc),
        in_specs=[blk, blk, blk, blk, pl.BlockSpec((1, BRANCH_W), lambda b, i: (0, 0))],
        out_specs=[blk, pl.BlockSpec((1, R_HEADS, R_DK, R_DV), lambda b, i: (b, 0, 0, 0))],
        out_shape=[jax.ShapeDtypeStruct((n, BRANCH_W), BF16),
                   jax.ShapeDtypeStruct((nb, R_HEADS, R_DK, R_DV), F32)],
        scratch_shapes=[pltpu.VMEM((R_HEADS, R_DK, R_DV), F32), pltpu.VMEM((R_HEADS, c, c), F32),
                        pltpu.VMEM((R_HEADS, c, LANES), F32), pltpu.VMEM((R_HEADS, c, LANES), F32)],
        compiler_params=_cparams(("arbitrary", "arbitrary")),
        name="ret_prompt",
    )(mix['rq'], mix['rk'], mix['rv'], mix['srg'], p['ret_gain'])


def _ret_sample_kernel(q_ref, k_ref, v_ref, srg_ref, gain_ref, s0_ref, o_ref, st_ref, *, nb):
    for b in range(nb):
        for h in range(R_HEADS):
            sl = slice(h * LANES, (h + 1) * LANES)
            g = math.exp(RET_LOG_G[h])
            q = q_ref[b, :, sl]
            k = k_ref[b, :, sl]
            v = v_ref[b, :, sl]
            s0 = s0_ref[b, h]
            qcol = jnp.transpose(jnp.broadcast_to(q, (SUBLANES, LANES)))[:, 0:1]
            kcol = jnp.transpose(jnp.broadcast_to(k, (SUBLANES, LANES)))[:, 0:1]
            o = jnp.sum(q * k, axis=-1, keepdims=True) * v + g * jnp.sum(qcol * s0, axis=0, keepdims=True)
            st_ref[b, h] = g * s0 + kcol * v
            o_ref[b, :, sl] = _ret_finish(o, gain_ref[:, sl], srg_ref[b, :, sl].astype(F32)).astype(BF16)


def _ret_sample(mix, p, state, layer, nb_blk):
    nd = state.shape[1]
    r3 = lambda a: a.reshape(nd, 1, BRANCH_W)
    tok = pl.BlockSpec((nb_blk, 1, BRANCH_W), lambda i: (i, 0, 0))
    st = pl.BlockSpec((nb_blk, R_HEADS, R_DK, R_DV), lambda i: (i, 0, 0, 0))
    st_in = pl.BlockSpec((None, nb_blk, R_HEADS, R_DK, R_DV), lambda i: (layer, i, 0, 0, 0))
    o, s1 = pl.pallas_call(
        functools.partial(_ret_sample_kernel, nb=nb_blk),
        grid=(nd // nb_blk,),
        in_specs=[tok, tok, tok, tok, pl.BlockSpec((1, BRANCH_W), lambda i: (0, 0)), st_in],
        out_specs=[tok, st],
        out_shape=[jax.ShapeDtypeStruct((nd, 1, BRANCH_W), BF16),
                   jax.ShapeDtypeStruct((nd, R_HEADS, R_DK, R_DV), F32)],
        compiler_params=_cparams(("arbitrary",)),
        name="ret_sample",
    )(r3(mix['rq']), r3(mix['rk']), r3(mix['rv']), r3(mix['srg']), p['ret_gain'], state)
    return o.reshape(nd, BRANCH_W), s1


S5_LANE_BLK = 512
S5_HALF_W = BRANCH_W // 2
S5_HALF_CH = S_CH // 2


def _s5_cols(cb):
    per_half = S5_HALF_CH // S5_LANE_BLK
    base = (cb // per_half) * 2 * S5_HALF_CH + (cb % per_half) * S5_LANE_BLK
    return slice(base, base + S5_LANE_BLK), slice(base + S5_HALF_CH, base + S5_HALF_CH + S5_LANE_BLK)


def _cmul_add(ar, ai, hr, hi, br, bi):
    return ar * hr - ai * hi + br, ar * hi + ai * hr + bi


def _s5_prompt_kernel(u_ref, bmat_ref, cmat_ref, d_ref, ar_ref, ai_ref, atr_ref, ati_ref,
                      y_ref, hre_ref, him_ref, up_scr, bu_scr, yp_scr, car_scr, *, tm):
    it = pl.program_id(1)
    tp = tm // SUBLANES

    @pl.when(it == 0)
    def _():
        car_scr[...] = jnp.zeros(car_scr.shape, F32)

    u = u_ref[...].astype(F32)
    nlb = BRANCH_W // LANES
    for j in range(SUBLANES):
        for kb in range(nlb):
            up_scr[kb, pl.ds(j, tp, stride=SUBLANES), :] = u[j * tp:(j + 1) * tp, kb * LANES:(kb + 1) * LANES]
    up = jnp.concatenate([up_scr[kb] for kb in range(nlb)], axis=1)
    for b in range(2):
        bu_scr[:, b * 2 * S5_HALF_CH:(b + 1) * 2 * S5_HALF_CH] = jnp.dot(
            up[:, b * S5_HALF_W:(b + 1) * S5_HALF_W].astype(BF16), bmat_ref[b], preferred_element_type=F32)

    for cb in range(S_CH // S5_LANE_BLK):
        nat = slice(cb * S5_LANE_BLK, (cb + 1) * S5_LANE_BLK)
        re, im = _s5_cols(cb)
        ar = jnp.broadcast_to(ar_ref[:, nat], (SUBLANES, S5_LANE_BLK))
        ai = jnp.broadcast_to(ai_ref[:, nat], (SUBLANES, S5_LANE_BLK))

        def local_step(i, carry):
            hr, hi = carry
            r0 = pl.multiple_of(i * SUBLANES, SUBLANES)
            return _cmul_add(ar, ai, hr, hi, bu_scr[pl.ds(r0, SUBLANES), re], bu_scr[pl.ds(r0, SUBLANES), im])

        zero = jnp.zeros((SUBLANES, S5_LANE_BLK), F32)
        er, ei = lax.fori_loop(0, tp, local_step, (zero, zero))

        atr = atr_ref[:, nat]
        ati = ati_ref[:, nat]
        cr = car_scr[0:1, re]
        ci = car_scr[0:1, im]
        rows_r, rows_i = [cr], [ci]
        for j in range(1, SUBLANES):
            cr, ci = _cmul_add(atr, ati, cr, ci, er[j - 1:j], ei[j - 1:j])
            rows_r.append(cr)
            rows_i.append(ci)
        cr, ci = _cmul_add(atr, ati, cr, ci, er[SUBLANES - 1:SUBLANES], ei[SUBLANES - 1:SUBLANES])
        car_scr[0:1, re] = cr
        car_scr[0:1, im] = ci
        sr = jnp.concatenate(rows_r, axis=0)
        si = jnp.concatenate(rows_i, axis=0)

        def full_step(i, carry):
            hr, hi = carry
            r0 = pl.multiple_of(i * SUBLANES, SUBLANES)
            hr, hi = _cmul_add(ar, ai, hr, hi, bu_scr[pl.ds(r0, SUBLANES), re], bu_scr[pl.ds(r0, SUBLANES), im])
            bu_scr[pl.ds(r0, SUBLANES), re] = hr
            bu_scr[pl.ds(r0, SUBLANES), im] = hi
            return hr, hi

        lax.fori_loop(0, tp, full_step, (sr, si))

    yp = jnp.concatenate(
        [jnp.dot(bu_scr[:, b * 2 * S5_HALF_CH:(b + 1) * 2 * S5_HALF_CH].astype(BF16), cmat_ref[b],
                 preferred_element_type=F32) for b in range(2)], axis=1) + d_ref[...] * up
    for kb in range(nlb):
        yp_scr[kb] = yp[:, kb * LANES:(kb + 1) * LANES]
    for j in range(SUBLANES):
        for kb in range(nlb):
            y_ref[j * tp:(j + 1) * tp, kb * LANES:(kb + 1) * LANES] = _gelu_tanh(
                yp_scr[kb, pl.ds(j, tp, stride=SUBLANES), :]).astype(BF16)

    @pl.when(it == pl.num_programs(1) - 1)
    def _():
        hc = S5_HALF_CH
        hre_ref[0] = jnp.concatenate([car_scr[0:1, 0:hc], car_scr[0:1, 2 * hc:3 * hc]], axis=1)
        him_ref[0] = jnp.concatenate([car_scr[0:1, hc:2 * hc], car_scr[0:1, 3 * hc:4 * hc]], axis=1)


def _s5_prompt(mix, s5, nb, seq, tm):
    n = nb * seq
    nt = seq // tm
    row = lambda b, i: (b * nt + i, 0)
    const = lambda b, i: (0, 0)
    blk = pl.BlockSpec((tm, BRANCH_W), row)
    vec = pl.BlockSpec((1, S_CH), const)
    st = pl.BlockSpec((1, 1, S_CH), lambda b, i: (b, 0, 0))
    y, hre, him = pl.pallas_call(
        functools.partial(_s5_prompt_kernel, tm=tm),
        grid=(nb, nt),
        in_specs=[blk, pl.BlockSpec((2, S5_HALF_W, 2 * S5_HALF_CH), lambda b, i: (0, 0, 0)),
                  pl.BlockSpec((2, 2 * S5_HALF_CH, S5_HALF_W), lambda b, i: (0, 0, 0)),
                  pl.BlockSpec((1, BRANCH_W), const), vec, vec, vec, vec],
        out_specs=[blk, st, st],
        out_shape=[jax.ShapeDtypeStruct((n, BRANCH_W), BF16),
                   jax.ShapeDtypeStruct((nb, 1, S_CH), F32), jax.ShapeDtypeStruct((nb, 1, S_CH), F32)],
        scratch_shapes=[pltpu.VMEM((BRANCH_W // LANES, tm, LANES), F32), pltpu.VMEM((tm, 2 * S_CH), F32),
                        pltpu.VMEM((BRANCH_W // LANES, tm, LANES), F32), pltpu.VMEM((SUBLANES, 2 * S_CH), F32)],
        compiler_params=_cparams(("arbitrary", "arbitrary")),
        name="s5_prompt",
    )(mix['su'], s5['bmat'], s5['cmat'], s5['d'], s5['a_re'], s5['a_im'], s5['at_re'][tm // SUBLANES],
      s5['at_im'][tm // SUBLANES])
    return y, hre.reshape(nb, S_GROUPS, S_STATE), him.reshape(nb, S_GROUPS, S_STATE)


def _s5_sample_kernel(u_ref, bmat_ref, cmat_ref, d_ref, ar_ref, ai_ref, h0r_ref, h0i_ref, y_ref, hr_ref, hi_ref):
    u = u_ref[...]
    hc = S5_HALF_CH
    ys = []
    for b in range(2):
        ch = slice(b * hc, (b + 1) * hc)
        bu = jnp.dot(u[:, b * S5_HALF_W:(b + 1) * S5_HALF_W], bmat_ref[b], preferred_element_type=F32)
        hr, hi = _cmul_add(ar_ref[:, ch], ai_ref[:, ch], h0r_ref[:, ch], h0i_ref[:, ch], bu[:, 0:hc], bu[:, hc:2 * hc])
        hr_ref[:, ch] = hr
        hi_ref[:, ch] = hi
        h = jnp.concatenate([hr, hi], axis=-1).astype(BF16)
        ys.append(jnp.dot(h, cmat_ref[b], preferred_element_type=F32))
    y = jnp.concatenate(ys, axis=-1) + d_ref[...] * u.astype(F32)
    y_ref[...] = _gelu_tanh(y).astype(BF16)


def _s5_sample(mix, s5, h0_re, h0_im):
    nd = h0_re.shape[0]
    y, hr, hi = pl.pallas_call(
        _s5_sample_kernel,
        out_shape=[jax.ShapeDtypeStruct((nd, BRANCH_W), BF16),
                   jax.ShapeDtypeStruct((nd, S_CH), F32), jax.ShapeDtypeStruct((nd, S_CH), F32)],
        compiler_params=pltpu.CompilerParams(vmem_limit_bytes=VMEM_LIMIT),
        name="s5_sample",
    )(mix['su'], s5['bmat'], s5['cmat'], s5['d'], s5['a_re'], s5['a_im'],
      h0_re.reshape(nd, S_CH), h0_im.reshape(nd, S_CH))
    return y, hr.reshape(nd, S_GROUPS, S_STATE), hi.reshape(nd, S_GROUPS, S_STATE)


def _merge_kernel(x_ref, g_ref, wga_ref, wgr_ref, wgs_ref, oa_ref, or_ref, ys_ref, ssg_ref, wba_ref, wbr_ref,
                  wbs_ref, wglu_ref, bglu_ref, wout_ref, out_ref):
    x = x_ref[...]
    ms = jnp.mean(x * x, axis=-1, keepdims=True)
    h = (x * lax.rsqrt(ms + EPS) * g_ref[...]).astype(BF16)

    def gate(w_ref):
        return jax.nn.sigmoid(jnp.dot(h, w_ref[...], preferred_element_type=F32))

    ya = jnp.dot(oa_ref[...], wba_ref[...], preferred_element_type=F32)
    yr = jnp.dot(or_ref[...], wbr_ref[...], preferred_element_type=F32)
    glu = jnp.dot(ys_ref[...], wglu_ref[...], preferred_element_type=F32) + bglu_ref[...]
    s5 = glu[:, 0:BRANCH_W] * jax.nn.sigmoid(glu[:, BRANCH_W:2 * BRANCH_W]) * ssg_ref[...].astype(F32)
    y5 = jnp.dot(s5.astype(BF16), wbs_ref[...], preferred_element_type=F32)
    merged = gate(wga_ref) * ya + gate(wgr_ref) * yr + gate(wgs_ref) * y5
    out_ref[...] = x + jnp.dot(merged.astype(BF16), wout_ref[...], preferred_element_type=F32)


def _merge(x, o_attn, o_ret, y_s5, ssg, p, w_all, layer, tm):
    n = x.shape[0]
    row = lambda i: (i, 0)
    const = lambda i: (0, 0)
    half = pl.BlockSpec((tm, BRANCH_W), row)
    full = lambda a: pl.BlockSpec(a.shape, const)
    ws = [p['w_br_attn'], p['w_br_ret'], p['w_br_s5'], p['w_glu'], p['b_glu'], p['w_out']]
    gate_blk = lambda j, i: (layer, 0, N_MIX // D_MODEL + j)
    return pl.pallas_call(
        _merge_kernel,
        grid=(n // tm,),
        in_specs=[pl.BlockSpec((tm, D_MODEL), row), full(p['norm_gain'])]
                 + [pl.BlockSpec((None, D_MODEL, D_MODEL), functools.partial(gate_blk, j)) for j in range(N_BRANCH)]
                 + [half, half, half, half] + [full(w) for w in ws],
        out_specs=pl.BlockSpec((tm, D_MODEL), row),
        out_shape=jax.ShapeDtypeStruct((n, D_MODEL), F32),
        compiler_params=_cparams(("arbitrary",)),
        name="merge",
    )(x, p['norm_gain'], w_all, w_all, w_all, o_attn, o_ret, y_s5, ssg, *ws)


def _rot_tables(pos, rows):
    pos = np.asarray(pos, np.float32)[:, None]

    def tab(half, reps):
        inv = (ROPE_THETA ** (-np.arange(half, dtype=np.float32) / half)).astype(np.float32)
        ang = (pos * inv[None, :]).astype(np.float32)
        cos, sin = np.cos(ang), np.sin(ang)
        c = np.tile(np.concatenate([cos, cos], axis=-1), (1, reps))
        s = np.tile(np.concatenate([-sin, sin], axis=-1), (1, reps))
        c, s = np.broadcast_to(c, (rows, LANES)), np.broadcast_to(s, (rows, LANES))
        return jnp.asarray(c, F32), jnp.asarray(s, F32)

    cosa, sina = tab(A_DH // 2, LANES // A_DH)
    cosr, sinr = tab(R_DK // 2, 1)
    return dict(cosa=cosa, sina=sina, cosr=cosr, sinr=sinr)


def _s5_params(lam_re, lam_im, log_dt, b_re, b_im, c_re, c_im, d_skip, powers):
    dt = jnp.exp(log_dt.astype(F32))[:, None]
    lre, lim = lam_re.astype(F32), lam_im.astype(F32)
    mag = jnp.exp(lre * dt)
    ab_re, ab_im = mag * jnp.cos(lim * dt), mag * jnp.sin(lim * dt)
    den = lre * lre + lim * lim
    nre = ab_re - 1.0
    cre = (nre * lre + ab_im * lim) / den
    cim = (ab_im * lre - nre * lim) / den
    b_re, b_im = b_re.astype(F32), b_im.astype(F32)
    bb_re = cre[..., None] * b_re - cim[..., None] * b_im
    bb_im = cre[..., None] * b_im + cim[..., None] * b_re
    same_group = (np.arange(BRANCH_W)[:, None] // S_GROUP == np.arange(S_CH)[None, :] // S_STATE)
    mask_in = jnp.asarray(same_group, F32)
    mask_out = jnp.asarray(same_group.T, F32)

    def in_mat(bb):
        return jnp.tile(jnp.swapaxes(bb, 1, 2).reshape(BRANCH_W, S_STATE), (1, S_GROUPS)) * mask_in

    def out_mat(cc):
        return jnp.tile(jnp.swapaxes(cc.astype(F32), 1, 2).reshape(S_CH, S_GROUP), (1, S_GROUPS)) * mask_out

    def halves(m, rows, cols):
        return [m[b * rows:(b + 1) * rows, b * cols:(b + 1) * cols] for b in range(2)]

    bmat = jnp.stack([jnp.concatenate([r, i], axis=1) for r, i in
                      zip(halves(in_mat(bb_re), S5_HALF_W, S5_HALF_CH),
                          halves(in_mat(bb_im), S5_HALF_W, S5_HALF_CH))]).astype(BF16)
    cmat = jnp.stack([jnp.concatenate([r, -i], axis=0) for r, i in
                      zip(halves(out_mat(c_re), S5_HALF_CH, S5_HALF_W),
                          halves(out_mat(c_im), S5_HALF_CH, S5_HALF_W))]).astype(BF16)
    a_re, a_im = ab_re.reshape(1, S_CH), ab_im.reshape(1, S_CH)
    at_re, at_im = {1: a_re}, {1: a_im}
    pr, pi, k = a_re, a_im, 1
    while k < max(powers):
        pr, pi, k = pr * pr - pi * pi, 2.0 * pr * pi, 2 * k
        at_re[k], at_im[k] = pr, pi
    return dict(bmat=bmat, cmat=cmat, d=d_skip.astype(F32).reshape(1, BRANCH_W), a_re=a_re, a_im=a_im,
                at_re=at_re, at_im=at_im)


TM_PROJ = 512
TQ_ATTN = 1024
RC_ATTN = 256
TQ_ATTN_RUNNING_MAX = 512
RET_CHUNK = 256
TM_S5 = 512
TM_MERGE = 512
SAMPLE_PAGES = 16
RET_SAMPLE_BLK = 8


def kernel(x_prompt, x_sample, cache_k, cache_v, state_ret, state_s5_re, state_s5_im, page_table, norm_gain, w_in, q_norm_gain, k_norm_gain, lam_q1, lam_k1, lam_q2, lam_k2, attn_out_gain, ret_out_gain, s5_lam_re, s5_lam_im, s5_log_dt, s5_b_re, s5_b_im, s5_c_re, s5_c_im, s5_d, w_glu, b_glu, w_br_attn, w_br_ret, w_br_s5, w_out):
    depth = w_in.shape[0]
    nb, seq, _ = x_prompt.shape
    nd, dseq, _ = x_sample.shape
    assert dseq == 1, "the sample group is a single-token step"
    n_pages = page_table.shape[1]
    past_len = n_pages * PAGE_SIZE
    tm_proj, tq, c_ret = min(TM_PROJ, seq), min(TQ_ATTN, seq), min(RET_CHUNK, seq)
    tm_s5, tm_merge = min(TM_S5, seq), min(TM_MERGE, seq)
    pages = min(SAMPLE_PAGES, n_pages)
    assert seq % tm_proj == 0 and seq % tq == 0 and seq % min(TQ_ATTN_RUNNING_MAX, seq) == 0 and seq % c_ret == 0 and seq % tm_s5 == 0 and seq % tm_merge == 0
    assert n_pages % pages == 0 and nd % RET_SAMPLE_BLK == 0

    tabs_p = _rot_tables(np.arange(seq), seq)
    tabs_s = _rot_tables(past_len + np.arange(1), nd)
    grp = (jnp.arange(LANES)[:, None] // A_DH == jnp.arange(LANES)[None, :] // A_DH).astype(BF16)

    yp = x_prompt.reshape(nb * seq, D_MODEL)
    ys = x_sample.reshape(nd, D_MODEL)
    outs = {k: [] for k in ('rp', 'spr', 'spi', 'rs', 'ssr', 'ssi')}
    w_all = w_in.astype(BF16)
    kv_p = kv_s = None
    for l in range(depth):
        lam_init = 0.8 - 0.6 * math.exp(-0.3 * l)
        p = dict(
            norm_gain=norm_gain[l].reshape(1, D_MODEL), grp=grp,
            q_gain=jnp.tile(q_norm_gain[l], LANES // A_DH).reshape(1, LANES),
            k_gain=jnp.tile(k_norm_gain[l], LANES // A_DH).reshape(1, LANES),
            lam_q1=lam_q1[l].reshape(1, A_DH), lam_k1=lam_k1[l].reshape(1, A_DH),
            lam_q2=lam_q2[l].reshape(1, A_DH), lam_k2=lam_k2[l].reshape(1, A_DH),
            attn_gain=attn_out_gain[l].reshape(1, BRANCH_W), ret_gain=ret_out_gain[l].reshape(1, BRANCH_W),
            w_glu=w_glu[l].astype(BF16), b_glu=b_glu[l].reshape(1, 2 * BRANCH_W),
            w_br_attn=w_br_attn[l].astype(BF16), w_br_ret=w_br_ret[l].astype(BF16),
            w_br_s5=w_br_s5[l].astype(BF16), w_out=w_out[l].astype(BF16))
        s5 = _s5_params(s5_lam_re[l], s5_lam_im[l], s5_log_dt[l], s5_b_re[l], s5_b_im[l], s5_c_re[l], s5_c_im[l],
                        s5_d[l], (tm_s5 // SUBLANES,))

        mix = _inproj(yp, p, w_all, tabs_p, tm_proj, seq, BF16, l, depth, kv_p)
        kv_p = (mix['kf'], mix['vf'])
        o_attn = _attn_prompt(mix, p, nb, seq, lam_init, tq, min(RC_ATTN, tq), min(TQ_ATTN_RUNNING_MAX, seq))
        o_ret, ret_state = _ret_prompt(mix, p, nb, seq, c_ret)
        y_s5, s_re, s_im = _s5_prompt(mix, s5, nb, seq, tm_s5)
        yp = _merge(yp, o_attn, o_ret, y_s5, mix['ssg'], p, w_all, l, tm_merge)
        outs['rp'].append(ret_state)
        outs['spr'].append(s_re)
        outs['spi'].append(s_im)

        mix = _inproj(ys, p, w_all, tabs_s, nd, nd, F32, l, depth, kv_s)
        kv_s = (mix['kf'], mix['vf'])
        o_attn = _attn_sample(mix, p, cache_k, cache_v, page_table, l, lam_init, pages)
        o_ret, ret_state = _ret_sample(mix, p, state_ret, l, RET_SAMPLE_BLK)
        y_s5, s_re, s_im = _s5_sample(mix, s5, state_s5_re[l], state_s5_im[l])
        ys = _merge(ys, o_attn, o_ret, y_s5, mix['ssg'], p, w_all, l, nd)
        outs['rs'].append(ret_state)
        outs['ssr'].append(s_re)
        outs['ssi'].append(s_im)

    st = lambda k: jnp.stack(outs[k])
    kv5 = lambda a, b, t: a.reshape(depth, b, t, A_HEADS, A_DV)
    return (yp.reshape(nb, seq, D_MODEL), ys.reshape(nd, 1, D_MODEL),
            kv5(kv_p[0], nb, seq), kv5(kv_p[1], nb, seq), st('rp'), st('spr'), st('spi'),
            kv5(kv_s[0], nd, 1), kv5(kv_s[1], nd, 1), st('rs'), st('ssr'), st('ssi'))
```

```python
import functools
import math

import jax
import jax.numpy as jnp
import numpy as np
from jax import lax
from jax.experimental import pallas as pl
from jax.experimental.pallas import tpu as pltpu

F32 = jnp.float32
BF16 = jnp.bfloat16

LANES = 128
SUBLANES = 8
VMEM_LIMIT = 56 * 1024 * 1024

D_MODEL = 1024
BRANCH_W = D_MODEL // 2
A_HEADS = 4
A_DH = BRANCH_W // (2 * A_HEADS)
A_DV = 2 * A_DH
R_HEADS = 4
R_DK = BRANCH_W // R_HEADS
R_DV = BRANCH_W // R_HEADS
S_GROUP = 16
S_GROUPS = BRANCH_W // S_GROUP
S_STATE = 64
S_CH = S_GROUPS * S_STATE
N_BRANCH = 3
N_MIX = 10 * BRANCH_W
PAGE_SIZE = 128
ROPE_THETA = 10000.0
EPS = 1e-6
LOG2E = math.log2(math.e)
RET_LOG_G = tuple(math.log1p(-2.0 ** (-5.0 - h)) for h in range(R_HEADS))

SEC_AQ, SEC_AK, SEC_AV, SEC_AG, SEC_RQ, SEC_RK, SEC_RV, SEC_RG, SEC_SU, SEC_SG = range(10)


def _cparams(sem):
    return pltpu.CompilerParams(dimension_semantics=sem, vmem_limit_bytes=VMEM_LIMIT)


def _silu(x):
    return x * jax.nn.sigmoid(x)


def _gelu_tanh(x):
    return 0.5 * x * (1.0 + jnp.tanh(math.sqrt(2.0 / math.pi) * (x + 0.044715 * x * x * x)))


N_INPROJ_IN = 10


def _inproj_kernel(*refs):
    x_ref, g_ref, w_ref, cosa_ref, sina_ref, cosr_ref, sinr_ref, qg_ref, kg_ref, grp_ref = refs[:N_INPROJ_IN]
    (q_ref, kf_ref, vf_ref, kb_ref, vb_ref, sag_ref, rq_ref, rk_ref, rv_ref, srg_ref,
     su_ref, ssg_ref) = refs[-12:]
    x = x_ref[...]
    tm = x.shape[0]
    ms = jnp.mean(x * x, axis=-1, keepdims=True)
    hn = (x * lax.rsqrt(ms + EPS) * g_ref[...]).astype(BF16)
    lane = lax.broadcasted_iota(jnp.int32, (tm, LANES), 1)

    def section(s):
        return jnp.dot(hn, w_ref[:, s * BRANCH_W:(s + 1) * BRANCH_W], preferred_element_type=F32)

    def head(t, h):
        return t[:, h * LANES:(h + 1) * LANES]

    def qk_norm_rot(zh, gain):
        ss = jnp.dot((zh * zh).astype(BF16), grp_ref[...], preferred_element_type=F32)
        y = zh * lax.rsqrt(ss * (1.0 / A_DH) + EPS) * gain
        half = A_DH // 2
        swapped = jnp.where((lane & half) == 0, pltpu.roll(y, LANES - half, 1), pltpu.roll(y, half, 1))
        return y * cosa_ref[...] + swapped * sina_ref[...]

    def ret_rot(zh):
        return zh * cosr_ref[...] + pltpu.roll(zh, R_DK // 2, 1) * sinr_ref[...]

    def head_rows(h):
        return pl.ds(h, tm, stride=A_HEADS)

    z = section(SEC_AQ)
    for h in range(A_HEADS):
        y = qk_norm_rot(head(z, h), qg_ref[...]) * (A_DH ** -0.5 * LOG2E)
        q_ref[:, h * LANES:(h + 1) * LANES] = y.astype(BF16)

    z = section(SEC_AK)
    for h in range(A_HEADS):
        y = qk_norm_rot(head(z, h), kg_ref[...])
        kf_ref[head_rows(h), :] = y
        kb_ref[:, h * LANES:(h + 1) * LANES] = y.astype(BF16)

    z = section(SEC_AV)
    for h in range(A_HEADS):
        vf_ref[head_rows(h), :] = head(z, h)
    vb_ref[...] = z.astype(BF16)

    sag_ref[...] = _silu(section(SEC_AG)).astype(BF16)

    z = section(SEC_RQ)
    for h in range(R_HEADS):
        rq_ref[:, h * LANES:(h + 1) * LANES] = ret_rot(head(z, h)).astype(rq_ref.dtype)

    z = section(SEC_RK)
    for h in range(R_HEADS):
        rk_ref[:, h * LANES:(h + 1) * LANES] = (ret_rot(head(z, h)) * (R_DK ** -0.5)).astype(rk_ref.dtype)

    rv_ref[...] = section(SEC_RV).astype(rv_ref.dtype)
    srg_ref[...] = _silu(section(SEC_RG)).astype(BF16)
    su_ref[...] = section(SEC_SU).astype(BF16)
    ssg_ref[...] = _silu(section(SEC_SG)).astype(BF16)


def _inproj(x, p, w_all, tabs, tm, rows_per_seq, ret_dtype, layer, depth, kv_prev):
    n = x.shape[0]
    nblk = n // tm
    tab_blocks = rows_per_seq // tm if rows_per_seq >= tm else 1
    row = lambda i: (i, 0)
    tab = lambda i: (i % tab_blocks, 0)
    const = lambda i: (0, 0)
    sec = pl.BlockSpec((tm, BRANCH_W), row)
    hrow = pl.BlockSpec((tm * A_HEADS, A_DV), lambda i: (layer * nblk + i, 0))
    tspec = pl.BlockSpec((tm, LANES), tab)
    names = ['q', 'kf', 'vf', 'kb', 'vb', 'sag', 'rq', 'rk', 'rv', 'srg', 'su', 'ssg']
    dts = [BF16, F32, F32, BF16, BF16, BF16, ret_dtype, ret_dtype, ret_dtype, BF16, BF16, BF16]
    shapes = [(depth * n * A_HEADS, A_DV) if nm in ('kf', 'vf') else (n, BRANCH_W) for nm in names]
    in_specs = [pl.BlockSpec((tm, D_MODEL), row),
                pl.BlockSpec((1, D_MODEL), const),
                pl.BlockSpec((None, D_MODEL, N_MIX), lambda i: (layer, 0, 0), pipeline_mode=pl.Buffered(1)),
                tspec, tspec, tspec, tspec,
                pl.BlockSpec((1, LANES), const), pl.BlockSpec((1, LANES), const),
                pl.BlockSpec((LANES, LANES), const)]
    args = [x, p['norm_gain'], w_all, tabs['cosa'], tabs['sina'], tabs['cosr'], tabs['sinr'],
            p['q_gain'], p['k_gain'], p['grp']]
    assert len(args) == N_INPROJ_IN
    aliases = {}
    if kv_prev is not None:
        in_specs += [pl.BlockSpec(memory_space=pl.ANY)] * 2
        args += list(kv_prev)
        aliases = {N_INPROJ_IN: names.index('kf'), N_INPROJ_IN + 1: names.index('vf')}
    outs = pl.pallas_call(
        _inproj_kernel,
        grid=(nblk,),
        in_specs=in_specs,
        out_specs=[hrow if nm in ('kf', 'vf') else sec for nm in names],
        out_shape=[jax.ShapeDtypeStruct(sh, dt) for sh, dt in zip(shapes, dts)],
        input_output_aliases=aliases,
        compiler_params=_cparams(("arbitrary",)),
        name="inproj",
    )(*args)
    return dict(zip(names, outs))


def _diff_lambda(lq1, lk1, lq2, lk2, lam_init):
    return (jnp.exp(jnp.sum(lq1 * lk1, axis=-1, keepdims=True))
            - jnp.exp(jnp.sum(lq2 * lk2, axis=-1, keepdims=True)) + lam_init)


def _attn_finish(o, gain, sag, lam_init):
    ms = jnp.mean(o * o, axis=-1, keepdims=True)
    return o * lax.rsqrt(ms + EPS) * gain * (1.0 - lam_init) * sag


def _attn_prompt_kernel(qi_ref, ki_ref, q_ref, k_ref, v_ref, sag_ref, lq1_ref, lk1_ref, lq2_ref, lk2_ref,
                        gain_ref, o_ref, qs_scr, m_scr, l_scr, acc_scr, *, lam_init, tq):
    t = pl.program_id(1)
    qi = qi_ref[t]
    ki = ki_ref[t]
    lane = lax.broadcasted_iota(jnp.int32, (tq, LANES), 1)

    @pl.when(ki == 0)
    def _():
        for h in range(A_HEADS):
            qh = q_ref[:, h * LANES:(h + 1) * LANES]
            zero = jnp.zeros_like(qh)
            qs_scr[h, 0:tq, :] = jnp.where(lane < A_DH, qh, zero)
            qs_scr[h, tq:2 * tq, :] = jnp.where(lane >= A_DH, qh, zero)
        m_scr[...] = jnp.full(m_scr.shape, -jnp.inf, F32)
        l_scr[...] = jnp.zeros(l_scr.shape, F32)
        acc_scr[...] = jnp.zeros(acc_scr.shape, F32)

    def update(masked):
        for h in range(A_HEADS):
            kh = k_ref[:, h * LANES:(h + 1) * LANES]
            vh = v_ref[:, h * LANES:(h + 1) * LANES]
            s = lax.dot_general(qs_scr[h], kh, (((1,), (1,)), ((), ())), preferred_element_type=F32)
            if masked:
                r = lax.broadcasted_iota(jnp.int32, s.shape, 0)
                c = lax.broadcasted_iota(jnp.int32, s.shape, 1)
                r = jnp.where(r >= tq, r - tq, r)
                s = jnp.where(c <= r, s, -jnp.inf)
            m_prev = m_scr[h]
            m_new = jnp.maximum(m_prev, jnp.max(s, axis=-1, keepdims=True))
            alpha = jnp.exp2(m_prev - m_new)
            pexp = jnp.exp2(s - m_new)
            l_scr[h] = alpha * l_scr[h] + jnp.sum(pexp, axis=-1, keepdims=True)
            acc_scr[h] = alpha * acc_scr[h] + jnp.dot(pexp.astype(BF16), vh, preferred_element_type=F32)
            m_scr[h] = m_new

    @pl.when(ki < qi)
    def _():
        update(False)

    @pl.when(ki == qi)
    def _():
        update(True)
        lam = _diff_lambda(lq1_ref[...], lk1_ref[...], lq2_ref[...], lk2_ref[...], lam_init)
        for h in range(A_HEADS):
            acc = acc_scr[h]
            l = l_scr[h]
            o = acc[0:tq] / l[0:tq] - lam * (acc[tq:2 * tq] / l[tq:2 * tq])
            sl = slice(h * LANES, (h + 1) * LANES)
            o_ref[:, sl] = _attn_finish(o, gain_ref[:, sl], sag_ref[:, sl].astype(F32), lam_init).astype(BF16)


def _attn_prompt_shift_kernel(qi_ref, ki_ref, q_ref, k_ref, v_ref, sag_ref, lq1_ref, lk1_ref, lq2_ref, lk2_ref,
                              gain_ref, shift_ref, o_ref, qs_scr, acc_scr, *, lam_init, tq, rc):
    t = pl.program_id(1)
    qi = qi_ref[t]
    ki = ki_ref[t]
    lane = lax.broadcasted_iota(jnp.int32, (tq, LANES), 1)

    @pl.when(ki == 0)
    def _():
        for h in range(A_HEADS):
            qh = q_ref[:, h * LANES:(h + 1) * LANES]
            zero = jnp.zeros_like(qh)
            qs_scr[h, 0:tq, :] = jnp.where(lane < A_DH, qh, zero)
            qs_scr[h, tq:2 * tq, :] = jnp.where(lane >= A_DH, qh, zero)
        acc_scr[...] = jnp.zeros(acc_scr.shape, F32)

    shift = shift_ref[...]
    ones = jnp.ones((tq, LANES), BF16)

    def update(masked):
        for h in range(A_HEADS):
            sl = slice(h * LANES, (h + 1) * LANES)
            for r0 in range(0, 2 * tq, rc):
                kl = (r0 % tq) + rc if masked else tq
                vaug = jnp.concatenate([v_ref[0:kl, sl], ones[0:kl]], axis=1)
                s = lax.dot_general(qs_scr[h, r0:r0 + rc, :], k_ref[0:kl, sl], (((1,), (1,)), ((), ())),
                                    preferred_element_type=F32)
                if masked:
                    r = lax.broadcasted_iota(jnp.int32, s.shape, 0) + (r0 % tq)
                    c = lax.broadcasted_iota(jnp.int32, s.shape, 1)
                    s = jnp.where(c <= r, s, -jnp.inf)
                pexp = jnp.exp2(s - shift).astype(BF16)
                acc_scr[h, r0:r0 + rc, :] += jnp.dot(pexp, vaug, preferred_element_type=F32)

    @pl.when(ki < qi)
    def _():
        update(False)

    @pl.when(ki == qi)
    def _():
        update(True)
        lam = _diff_lambda(lq1_ref[...], lk1_ref[...], lq2_ref[...], lk2_ref[...], lam_init)
        for h in range(A_HEADS):
            a1 = acc_scr[h, 0:tq, :]
            a2 = acc_scr[h, tq:2 * tq, :]
            o = a1[:, 0:LANES] / a1[:, LANES:2 * LANES] - lam * (a2[:, 0:LANES] / a2[:, LANES:2 * LANES])
            sl = slice(h * LANES, (h + 1) * LANES)
            o_ref[:, sl] = _attn_finish(o, gain_ref[:, sl], sag_ref[:, sl].astype(F32), lam_init).astype(BF16)


ATTN_SHIFT_LIMIT = 48.0


def _attn_prompt(mix, p, nb, seq, lam_init, tq, rc, tq_running_max):
    n = nb * seq
    out_shape = jax.ShapeDtypeStruct((n, BRANCH_W), BF16)

    def call(kern, t, name, extra_specs, scratch, extra_args):
        nq = seq // t
        pairs = [(a, b) for a in range(nq) for b in range(a + 1)]
        qi = jnp.asarray(np.array([a for a, _ in pairs], np.int32))
        ki = jnp.asarray(np.array([b for _, b in pairs], np.int32))
        qmap = lambda b, s, qi, ki: (b * nq + qi[s], 0)
        kmap = lambda b, s, qi, ki: (b * nq + ki[s], 0)
        const = lambda b, s, qi, ki: (0, 0)
        blk = lambda m: pl.BlockSpec((t, BRANCH_W), m)
        vec = pl.BlockSpec((1, A_DH), const)
        return pl.pallas_call(
            kern,
            grid_spec=pltpu.PrefetchScalarGridSpec(
                num_scalar_prefetch=2, grid=(nb, len(pairs)),
                in_specs=[blk(qmap), blk(kmap), blk(kmap), blk(qmap), vec, vec, vec, vec,
                          pl.BlockSpec((1, BRANCH_W), const)] + [pl.BlockSpec(sh, const) for sh in extra_specs],
                out_specs=blk(qmap), scratch_shapes=scratch),
            out_shape=out_shape,
            compiler_params=_cparams(("arbitrary", "arbitrary")),
            name=name,
        )(qi, ki, mix['q'], mix['kb'], mix['vb'], mix['sag'], p['lam_q1'], p['lam_k1'], p['lam_q2'], p['lam_k2'],
          p['attn_gain'], *extra_args)

    def running_max(_):
        t = tq_running_max
        return call(functools.partial(_attn_prompt_kernel, lam_init=lam_init, tq=t), t, "attn_prompt", [],
                    [pltpu.VMEM((A_HEADS, 2 * t, LANES), BF16), pltpu.VMEM((A_HEADS, 2 * t, 1), F32),
                     pltpu.VMEM((A_HEADS, 2 * t, 1), F32), pltpu.VMEM((A_HEADS, 2 * t, LANES), F32)], [])

    def fixed_shift(shift):
        return call(functools.partial(_attn_prompt_shift_kernel, lam_init=lam_init, tq=tq, rc=rc), tq,
                    "attn_prompt_shift", [(1, 1)],
                    [pltpu.VMEM((A_HEADS, 2 * tq, LANES), BF16), pltpu.VMEM((A_HEADS, 2 * tq, 2 * LANES), F32)],
                    [shift])

    bound = (A_DH ** 0.5) * jnp.max(jnp.abs(p['q_gain'])) * jnp.max(jnp.abs(p['k_gain']))
    shift = (bound * LOG2E).reshape(1, 1).astype(F32)
    return lax.cond(2.0 * bound <= ATTN_SHIFT_LIMIT, fixed_shift, running_max, shift)


def _attn_sample_kernel(pt_ref, *refs, lam_init, pages):
    kc_refs = refs[0:pages]
    vc_refs = refs[pages:2 * pages]
    (qt_ref, kn_ref, vn_ref, sag_ref, lq1_ref, lk1_ref, lq2_ref, lk2_ref, gain_ref,
     o_ref, m_scr, l_scr, acc_scr) = refs[2 * pages:]
    s_id = pl.program_id(1)
    n_maps = 2 * A_HEADS
    qt = qt_ref[0]

    @pl.when(s_id == 0)
    def _():
        m_scr[...] = jnp.full(m_scr.shape, -jnp.inf, F32)
        l_scr[...] = jnp.zeros(l_scr.shape, F32)
        acc_scr[...] = jnp.zeros(acc_scr.shape, F32)

    kcat = jnp.concatenate([r[...].astype(BF16) for r in kc_refs], axis=0)
    vcat = jnp.concatenate([r[...].astype(BF16) for r in vc_refs], axis=0)
    s = lax.dot_general(qt.astype(BF16), kcat, (((1,), (1,)), ((), ())), preferred_element_type=F32)
    row = lax.broadcasted_iota(jnp.int32, s.shape, 0)
    col = lax.broadcasted_iota(jnp.int32, s.shape, 1)
    s = jnp.where(col % A_HEADS == row // 2, s, -jnp.inf)
    m_prev = m_scr[...]
    m_new = jnp.maximum(m_prev, jnp.max(s, axis=-1, keepdims=True))
    alpha = jnp.exp2(m_prev - m_new)
    pexp = jnp.exp2(s - m_new)
    l_scr[...] = alpha * l_scr[...] + jnp.sum(pexp, axis=-1, keepdims=True)
    acc_scr[...] = alpha * acc_scr[...] + jnp.dot(pexp.astype(BF16), vcat, preferred_element_type=F32)
    m_scr[...] = m_new

    @pl.when(s_id == pl.num_programs(1) - 1)
    def _():
        s_new = jnp.sum(qt * kn_ref[0], axis=-1, keepdims=True)
        m_prev = m_scr[...]
        m_fin = jnp.maximum(m_prev, s_new)
        alpha = jnp.exp2(m_prev - m_fin)
        p_new = jnp.exp2(s_new - m_fin)
        l = alpha * l_scr[...] + p_new
        acc = alpha * acc_scr[...] + p_new * vn_ref[0]
        lam = _diff_lambda(lq1_ref[...], lk1_ref[...], lq2_ref[...], lk2_ref[...], lam_init)
        r8 = lax.broadcasted_iota(jnp.int32, (n_maps, 1), 0)
        on = acc * (jnp.where(r8 % 2 == 0, 1.0, -lam) / l)
        for h in range(A_HEADS):
            sl = slice(h * LANES, (h + 1) * LANES)
            o = on[2 * h:2 * h + 1] + on[2 * h + 1:2 * h + 2]
            o_ref[0, :, sl] = _attn_finish(o, gain_ref[:, sl], sag_ref[0, :, sl].astype(F32),
                                           lam_init).astype(BF16)


def _attn_sample(mix, p, cache_k, cache_v, page_table, layer, lam_init, pages):
    nd, n_pages = page_table.shape
    new_rows = slice(layer * nd * A_HEADS, (layer + 1) * nd * A_HEADS)
    steps = n_pages // pages
    pool = cache_k.shape[1]
    rows_pp = PAGE_SIZE * A_HEADS
    ck = cache_k.reshape(cache_k.shape[0], pool, rows_pp, A_DV)
    cv = cache_v.reshape(cache_v.shape[0], pool, rows_pp, A_DV)

    def page_spec(i):
        return pl.BlockSpec((None, None, rows_pp, A_DV),
                            lambda b, s, pt: (layer, pt[b * n_pages + s * pages + i], 0, 0))

    n_maps = 2 * A_HEADS
    tok = pl.BlockSpec((1, 1, BRANCH_W), lambda b, s, pt: (b, 0, 0))
    tok8 = pl.BlockSpec((1, n_maps, LANES), lambda b, s, pt: (b, 0, 0))
    const = lambda b, s, pt: (0, 0)
    vec = pl.BlockSpec((1, A_DH), const)
    half = (jnp.arange(LANES)[None, :] // A_DH == jnp.arange(2)[:, None]).astype(F32)
    qt = (mix['q'].astype(F32).reshape(nd, A_HEADS, 1, LANES) * half).reshape(nd, n_maps, LANES)
    rep = lambda a: jnp.repeat(a.reshape(nd, A_HEADS, LANES), 2, axis=1)
    out = pl.pallas_call(
        functools.partial(_attn_sample_kernel, lam_init=lam_init, pages=pages),
        grid_spec=pltpu.PrefetchScalarGridSpec(
            num_scalar_prefetch=1,
            grid=(nd, steps),
            in_specs=[page_spec(i) for i in range(pages)] * 2
                     + [tok8, tok8, tok8, tok, vec, vec, vec, vec, pl.BlockSpec((1, BRANCH_W), const)],
            out_specs=tok,
            scratch_shapes=[pltpu.VMEM((n_maps, 1), F32), pltpu.VMEM((n_maps, 1), F32),
                            pltpu.VMEM((n_maps, LANES), F32)]),
        out_shape=jax.ShapeDtypeStruct((nd, 1, BRANCH_W), BF16),
        compiler_params=_cparams(("arbitrary", "arbitrary")),
        name="attn_sample",
    )(page_table.reshape(-1), *([ck] * pages), *([cv] * pages),
      qt, rep(mix['kf'][new_rows]), rep(mix['vf'][new_rows]), mix['sag'].reshape(nd, 1, BRANCH_W),
      p['lam_q1'], p['lam_k1'], p['lam_q2'], p['lam_k2'], p['attn_gain'])
    return out.reshape(nd, BRANCH_W)


def _ret_finish(o, gain, srg):
    ms = jnp.mean(o * o, axis=-1, keepdims=True)
    return o * lax.rsqrt(ms + EPS) * gain * srg


def _ret_prompt_kernel(q_ref, k_ref, v_ref, srg_ref, gain_ref, o_ref, st_ref, s_scr, dm_scr, qd_scr, kd_scr, *, c):
    ci = pl.program_id(1)

    @pl.when(ci == 0)
    def _():
        s_scr[...] = jnp.zeros(s_scr.shape, F32)
        i = lax.broadcasted_iota(jnp.int32, (c, c), 0)
        jj = lax.broadcasted_iota(jnp.int32, (c, c), 1)
        d = (i - jj).astype(F32)
        r = lax.broadcasted_iota(jnp.int32, (c, LANES), 0).astype(F32)
        for h in range(R_HEADS):
            lg = RET_LOG_G[h]
            dm_scr[h] = jnp.where(i >= jj, jnp.exp(d * lg), 0.0)
            qd_scr[h] = jnp.exp((r + 1.0) * lg)
            kd_scr[h] = jnp.exp((c - 1.0 - r) * lg)

    for h in range(R_HEADS):
        sl = slice(h * LANES, (h + 1) * LANES)
        q = q_ref[:, sl]
        k = k_ref[:, sl]
        v = v_ref[:, sl]
        att = lax.dot_general(q, k, (((1,), (1,)), ((), ())), preferred_element_type=F32) * dm_scr[h]
        s_old = s_scr[h]
        o = (jnp.dot(att.astype(BF16), v, preferred_element_type=F32)
             + jnp.dot((q.astype(F32) * qd_scr[h]).astype(BF16), s_old.astype(BF16), preferred_element_type=F32))
        kd = (k.astype(F32) * kd_scr[h]).astype(BF16)
        s_new = math.exp(c * RET_LOG_G[h]) * s_old + lax.dot_general(
            kd, v, (((0,), (0,)), ((), ())), preferred_element_type=F32)
        s_scr[h] = s_new
        o_ref[:, sl] = _ret_finish(o, gain_ref[:, sl], srg_ref[:, sl].astype(F32)).astype(BF16)

    @pl.when(ci == pl.num_programs(1) - 1)
    def _():
        st_ref[0] = s_scr[...]


def _ret_prompt(mix, p, nb, seq, c):
    n = nb * seq
    nc = seq // c
    row = lambda b, i: (b * nc + i, 0)
    blk = pl.BlockSpec((c, BRANCH_W), row)
    return pl.pallas_call(
        functools.partial(_ret_prompt_kernel, c=c),
        grid=(nb, nc),
        in_specs=[blk, blk, blk, blk, pl.BlockSpec((1, BRANCH_W), lambda b, i: (0, 0))],
        out_specs=[blk, pl.BlockSpec((1, R_HEADS, R_DK, R_DV), lambda b, i: (b, 0, 0, 0))],
        out_shape=[jax.ShapeDtypeStruct((n, BRANCH_W), BF16),
                   jax.ShapeDtypeStruct((nb, R_HEADS, R_DK, R_DV), F32)],
        scratch_shapes=[pltpu.VMEM((R_HEADS, R_DK, R_DV), F32), pltpu.VMEM((R_HEADS, c, c), F32),
                        pltpu.VMEM((R_HEADS, c, LANES), F32), pltpu.VMEM((R_HEADS, c, LANES), F32)],
        compiler_params=_cparams(("arbitrary", "arbitrary")),
        name="ret_prompt",
    )(mix['rq'], mix['rk'], mix['rv'], mix['srg'], p['ret_gain'])


def _ret_sample_kernel(q_ref, k_ref, v_ref, srg_ref, gain_ref, s0_ref, o_ref, st_ref, *, nb):
    for b in range(nb):
        for h in range(R_HEADS):
            sl = slice(h * LANES, (h + 1) * LANES)
            g = math.exp(RET_LOG_G[h])
            q = q_ref[b, :, sl]
            k = k_ref[b, :, sl]
            v = v_ref[b, :, sl]
            s0 = s0_ref[b, h]
            qcol = jnp.transpose(jnp.broadcast_to(q, (SUBLANES, LANES)))[:, 0:1]
            kcol = jnp.transpose(jnp.broadcast_to(k, (SUBLANES, LANES)))[:, 0:1]
            o = jnp.sum(q * k, axis=-1, keepdims=True) * v + g * jnp.sum(qcol * s0, axis=0, keepdims=True)
            st_ref[b, h] = g * s0 + kcol * v
            o_ref[b, :, sl] = _ret_finish(o, gain_ref[:, sl], srg_ref[b, :, sl].astype(F32)).astype(BF16)


def _ret_sample(mix, p, state, layer, nb_blk):
    nd = state.shape[1]
    r3 = lambda a: a.reshape(nd, 1, BRANCH_W)
    tok = pl.BlockSpec((nb_blk, 1, BRANCH_W), lambda i: (i, 0, 0))
    st = pl.BlockSpec((nb_blk, R_HEADS, R_DK, R_DV), lambda i: (i, 0, 0, 0))
    st_in = pl.BlockSpec((None, nb_blk, R_HEADS, R_DK, R_DV), lambda i: (layer, i, 0, 0, 0))
    o, s1 = pl.pallas_call(
        functools.partial(_ret_sample_kernel, nb=nb_blk),
        grid=(nd // nb_blk,),
        in_specs=[tok, tok, tok, tok, pl.BlockSpec((1, BRANCH_W), lambda i: (0, 0)), st_in],
        out_specs=[tok, st],
        out_shape=[jax.ShapeDtypeStruct((nd, 1, BRANCH_W), BF16),
                   jax.ShapeDtypeStruct((nd, R_HEADS, R_DK, R_DV), F32)],
        compiler_params=_cparams(("arbitrary",)),
        name="ret_sample",
    )(r3(mix['rq']), r3(mix['rk']), r3(mix['rv']), r3(mix['srg']), p['ret_gain'], state)
    return o.reshape(nd, BRANCH_W), s1


S5_LANE_BLK = 512
S5_HALF_W = BRANCH_W // 2
S5_HALF_CH = S_CH // 2


def _s5_cols(cb):
    per_half = S5_HALF_CH // S5_LANE_BLK
    base = (cb // per_half) * 2 * S5_HALF_CH + (cb % per_half) * S5_LANE_BLK
    return slice(base, base + S5_LANE_BLK), slice(base + S5_HALF_CH, base + S5_HALF_CH + S5_LANE_BLK)


def _cmul_add(ar, ai, hr, hi, br, bi):
    return ar * hr - ai * hi + br, ar * hi + ai * hr + bi


def _s5_prompt_kernel(u_ref, bmat_ref, cmat_ref, d_ref, ar_ref, ai_ref, atr_ref, ati_ref,
                      y_ref, hre_ref, him_ref, up_scr, bu_scr, yp_scr, car_scr, *, tm):
    it = pl.program_id(1)
    tp = tm // SUBLANES

    @pl.when(it == 0)
    def _():
        car_scr[...] = jnp.zeros(car_scr.shape, F32)

    u = u_ref[...].astype(F32)
    nlb = BRANCH_W // LANES
    for j in range(SUBLANES):
        for kb in range(nlb):
            up_scr[kb, pl.ds(j, tp, stride=SUBLANES), :] = u[j * tp:(j + 1) * tp, kb * LANES:(kb + 1) * LANES]
    up = jnp.concatenate([up_scr[kb] for kb in range(nlb)], axis=1)
    for b in range(2):
        bu_scr[:, b * 2 * S5_HALF_CH:(b + 1) * 2 * S5_HALF_CH] = jnp.dot(
            up[:, b * S5_HALF_W:(b + 1) * S5_HALF_W].astype(BF16), bmat_ref[b], preferred_element_type=F32)

    for cb in range(S_CH // S5_LANE_BLK):
        nat = slice(cb * S5_LANE_BLK, (cb + 1) * S5_LANE_BLK)
        re, im = _s5_cols(cb)
        ar = jnp.broadcast_to(ar_ref[:, nat], (SUBLANES, S5_LANE_BLK))
        ai = jnp.broadcast_to(ai_ref[:, nat], (SUBLANES, S5_LANE_BLK))

        def local_step(i, carry):
            hr, hi = carry
            r0 = pl.multiple_of(i * SUBLANES, SUBLANES)
            return _cmul_add(ar, ai, hr, hi, bu_scr[pl.ds(r0, SUBLANES), re], bu_scr[pl.ds(r0, SUBLANES), im])

        zero = jnp.zeros((SUBLANES, S5_LANE_BLK), F32)
        er, ei = lax.fori_loop(0, tp, local_step, (zero, zero))

        atr = atr_ref[:, nat]
        ati = ati_ref[:, nat]
        cr = car_scr[0:1, re]
        ci = car_scr[0:1, im]
        rows_r, rows_i = [cr], [ci]
        for j in range(1, SUBLANES):
            cr, ci = _cmul_add(atr, ati, cr, ci, er[j - 1:j], ei[j - 1:j])
            rows_r.append(cr)
            rows_i.append(ci)
        cr, ci = _cmul_add(atr, ati, cr, ci, er[SUBLANES - 1:SUBLANES], ei[SUBLANES - 1:SUBLANES])
        car_scr[0:1, re] = cr
        car_scr[0:1, im] = ci
        sr = jnp.concatenate(rows_r, axis=0)
        si = jnp.concatenate(rows_i, axis=0)

        def full_step(i, carry):
            hr, hi = carry
            r0 = pl.multiple_of(i * SUBLANES, SUBLANES)
            hr, hi = _cmul_add(ar, ai, hr, hi, bu_scr[pl.ds(r0, SUBLANES), re], bu_scr[pl.ds(r0, SUBLANES), im])
            bu_scr[pl.ds(r0, SUBLANES), re] = hr
            bu_scr[pl.ds(r0, SUBLANES), im] = hi
            return hr, hi

        lax.fori_loop(0, tp, full_step, (sr, si))

    yp = jnp.concatenate(
        [jnp.dot(bu_scr[:, b * 2 * S5_HALF_CH:(b + 1) * 2 * S5_HALF_CH].astype(BF16), cmat_ref[b],
                 preferred_element_type=F32) for b in range(2)], axis=1) + d_ref[...] * up
    for kb in range(nlb):
        yp_scr[kb] = yp[:, kb * LANES:(kb + 1) * LANES]
    for j in range(SUBLANES):
        for kb in range(nlb):
            y_ref[j * tp:(j + 1) * tp, kb * LANES:(kb + 1) * LANES] = _gelu_tanh(
                yp_scr[kb, pl.ds(j, tp, stride=SUBLANES), :]).astype(BF16)

    @pl.when(it == pl.num_programs(1) - 1)
    def _():
        hc = S5_HALF_CH
        hre_ref[0] = jnp.concatenate([car_scr[0:1, 0:hc], car_scr[0:1, 2 * hc:3 * hc]], axis=1)
        him_ref[0] = jnp.concatenate([car_scr[0:1, hc:2 * hc], car_scr[0:1, 3 * hc:4 * hc]], axis=1)


def _s5_prompt(mix, s5, nb, seq, tm):
    n = nb * seq
    nt = seq // tm
    row = lambda b, i: (b * nt + i, 0)
    const = lambda b, i: (0, 0)
    blk = pl.BlockSpec((tm, BRANCH_W), row)
    vec = pl.BlockSpec((1, S_CH), const)
    st = pl.BlockSpec((1, 1, S_CH), lambda b, i: (b, 0, 0))
    y, hre, him = pl.pallas_call(
        functools.partial(_s5_prompt_kernel, tm=tm),
        grid=(nb, nt),
        in_specs=[blk, pl.BlockSpec((2, S5_HALF_W, 2 * S5_HALF_CH), lambda b, i: (0, 0, 0)),
                  pl.BlockSpec((2, 2 * S5_HALF_CH, S5_HALF_W), lambda b, i: (0, 0, 0)),
                  pl.BlockSpec((1, BRANCH_W), const), vec, vec, vec, vec],
        out_specs=[blk, st, st],
        out_shape=[jax.ShapeDtypeStruct((n, BRANCH_W), BF16),
                   jax.ShapeDtypeStruct((nb, 1, S_CH), F32), jax.ShapeDtypeStruct((nb, 1, S_CH), F32)],
        scratch_shapes=[pltpu.VMEM((BRANCH_W // LANES, tm, LANES), F32), pltpu.VMEM((tm, 2 * S_CH), F32),
                        pltpu.VMEM((BRANCH_W // LANES, tm, LANES), F32), pltpu.VMEM((SUBLANES, 2 * S_CH), F32)],
        compiler_params=_cparams(("arbitrary", "arbitrary")),
        name="s5_prompt",
    )(mix['su'], s5['bmat'], s5['cmat'], s5['d'], s5['a_re'], s5['a_im'], s5['at_re'][tm // SUBLANES],
      s5['at_im'][tm // SUBLANES])
    return y, hre.reshape(nb, S_GROUPS, S_STATE), him.reshape(nb, S_GROUPS, S_STATE)


def _s5_sample_kernel(u_ref, bmat_ref, cmat_ref, d_ref, ar_ref, ai_ref, h0r_ref, h0i_ref, y_ref, hr_ref, hi_ref):
    u = u_ref[...]
    hc = S5_HALF_CH
    ys = []
    for b in range(2):
        ch = slice(b * hc, (b + 1) * hc)
        bu = jnp.dot(u[:, b * S5_HALF_W:(b + 1) * S5_HALF_W], bmat_ref[b], preferred_element_type=F32)
        hr, hi = _cmul_add(ar_ref[:, ch], ai_ref[:, ch], h0r_ref[:, ch], h0i_ref[:, ch], bu[:, 0:hc], bu[:, hc:2 * hc])
        hr_ref[:, ch] = hr
        hi_ref[:, ch] = hi
        h = jnp.concatenate([hr, hi], axis=-1).astype(BF16)
        ys.append(jnp.dot(h, cmat_ref[b], preferred_element_type=F32))
    y = jnp.concatenate(ys, axis=-1) + d_ref[...] * u.astype(F32)
    y_ref[...] = _gelu_tanh(y).astype(BF16)


def _s5_sample(mix, s5, h0_re, h0_im):
    nd = h0_re.shape[0]
    y, hr, hi = pl.pallas_call(
        _s5_sample_kernel,
        out_shape=[jax.ShapeDtypeStruct((nd, BRANCH_W), BF16),
                   jax.ShapeDtypeStruct((nd, S_CH), F32), jax.ShapeDtypeStruct((nd, S_CH), F32)],
        compiler_params=pltpu.CompilerParams(vmem_limit_bytes=VMEM_LIMIT),
        name="s5_sample",
    )(mix['su'], s5['bmat'], s5['cmat'], s5['d'], s5['a_re'], s5['a_im'],
      h0_re.reshape(nd, S_CH), h0_im.reshape(nd, S_CH))
    return y, hr.reshape(nd, S_GROUPS, S_STATE), hi.reshape(nd, S_GROUPS, S_STATE)


def _merge_kernel(x_ref, g_ref, wga_ref, wgr_ref, wgs_ref, oa_ref, or_ref, ys_ref, ssg_ref, wba_ref, wbr_ref,
                  wbs_ref, wglu_ref, bglu_ref, wout_ref, out_ref):
    x = x_ref[...]
    ms = jnp.mean(x * x, axis=-1, keepdims=True)
    h = (x * lax.rsqrt(ms + EPS) * g_ref[...]).astype(BF16)

    def gate(w_ref):
        return jax.nn.sigmoid(jnp.dot(h, w_ref[...], preferred_element_type=F32))

    ya = jnp.dot(oa_ref[...], wba_ref[...], preferred_element_type=F32)
    yr = jnp.dot(or_ref[...], wbr_ref[...], preferred_element_type=F32)
    glu = jnp.dot(ys_ref[...], wglu_ref[...], preferred_element_type=F32) + bglu_ref[...]
    s5 = glu[:, 0:BRANCH_W] * jax.nn.sigmoid(glu[:, BRANCH_W:2 * BRANCH_W]) * ssg_ref[...].astype(F32)
    y5 = jnp.dot(s5.astype(BF16), wbs_ref[...], preferred_element_type=F32)
    merged = gate(wga_ref) * ya + gate(wgr_ref) * yr + gate(wgs_ref) * y5
    out_ref[...] = x + jnp.dot(merged.astype(BF16), wout_ref[...], preferred_element_type=F32)


def _merge(x, o_attn, o_ret, y_s5, ssg, p, w_all, layer, tm):
    n = x.shape[0]
    row = lambda i: (i, 0)
    const = lambda i: (0, 0)
    half = pl.BlockSpec((tm, BRANCH_W), row)
    full = lambda a: pl.BlockSpec(a.shape, const)
    ws = [p['w_br_attn'], p['w_br_ret'], p['w_br_s5'], p['w_glu'], p['b_glu'], p['w_out']]
    gate_blk = lambda j, i: (layer, 0, N_MIX // D_MODEL + j)
    return pl.pallas_call(
        _merge_kernel,
        grid=(n // tm,),
        in_specs=[pl.BlockSpec((tm, D_MODEL), row), full(p['norm_gain'])]
                 + [pl.BlockSpec((None, D_MODEL, D_MODEL), functools.partial(gate_blk, j)) for j in range(N_BRANCH)]
                 + [half, half, half, half] + [full(w) for w in ws],
        out_specs=pl.BlockSpec((tm, D_MODEL), row),
        out_shape=jax.ShapeDtypeStruct((n, D_MODEL), F32),
        compiler_params=_cparams(("arbitrary",)),
        name="merge",
    )(x, p['norm_gain'], w_all, w_all, w_all, o_attn, o_ret, y_s5, ssg, *ws)


def _rot_tables(pos, rows):
    pos = np.asarray(pos, np.float64)[:, None]

    def tab(half, reps):
        inv = ROPE_THETA ** (-np.arange(half, dtype=np.float64) / half)
        ang = pos * inv[None, :]
        cos, sin = np.cos(ang), np.sin(ang)
        c = np.tile(np.concatenate([cos, cos], axis=-1), (1, reps))
        s = np.tile(np.concatenate([-sin, sin], axis=-1), (1, reps))
        c, s = np.broadcast_to(c, (rows, LANES)), np.broadcast_to(s, (rows, LANES))
        return jnp.asarray(c, F32), jnp.asarray(s, F32)

    cosa, sina = tab(A_DH // 2, LANES // A_DH)
    cosr, sinr = tab(R_DK // 2, 1)
    return dict(cosa=cosa, sina=sina, cosr=cosr, sinr=sinr)


def _s5_params(lam_re, lam_im, log_dt, b_re, b_im, c_re, c_im, d_skip, powers):
    dt = jnp.exp(log_dt.astype(F32))[:, None]
    lre, lim = lam_re.astype(F32), lam_im.astype(F32)
    mag = jnp.exp(lre * dt)
    ab_re, ab_im = mag * jnp.cos(lim * dt), mag * jnp.sin(lim * dt)
    den = lre * lre + lim * lim
    nre = ab_re - 1.0
    cre = (nre * lre + ab_im * lim) / den
    cim = (ab_im * lre - nre * lim) / den
    b_re, b_im = b_re.astype(F32), b_im.astype(F32)
    bb_re = cre[..., None] * b_re - cim[..., None] * b_im
    bb_im = cre[..., None] * b_im + cim[..., None] * b_re
    same_group = (np.arange(BRANCH_W)[:, None] // S_GROUP == np.arange(S_CH)[None, :] // S_STATE)
    mask_in = jnp.asarray(same_group, F32)
    mask_out = jnp.asarray(same_group.T, F32)

    def in_mat(bb):
        return jnp.tile(jnp.swapaxes(bb, 1, 2).reshape(BRANCH_W, S_STATE), (1, S_GROUPS)) * mask_in

    def out_mat(cc):
        return jnp.tile(jnp.swapaxes(cc.astype(F32), 1, 2).reshape(S_CH, S_GROUP), (1, S_GROUPS)) * mask_out

    def halves(m, rows, cols):
        return [m[b * rows:(b + 1) * rows, b * cols:(b + 1) * cols] for b in range(2)]

    bmat = jnp.stack([jnp.concatenate([r, i], axis=1) for r, i in
                      zip(halves(in_mat(bb_re), S5_HALF_W, S5_HALF_CH),
                          halves(in_mat(bb_im), S5_HALF_W, S5_HALF_CH))]).astype(BF16)
    cmat = jnp.stack([jnp.concatenate([r, -i], axis=0) for r, i in
                      zip(halves(out_mat(c_re), S5_HALF_CH, S5_HALF_W),
                          halves(out_mat(c_im), S5_HALF_CH, S5_HALF_W))]).astype(BF16)
    a_re, a_im = ab_re.reshape(1, S_CH), ab_im.reshape(1, S_CH)
    at_re, at_im = {1: a_re}, {1: a_im}
    pr, pi, k = a_re, a_im, 1
    while k < max(powers):
        pr, pi, k = pr * pr - pi * pi, 2.0 * pr * pi, 2 * k
        at_re[k], at_im[k] = pr, pi
    return dict(bmat=bmat, cmat=cmat, d=d_skip.astype(F32).reshape(1, BRANCH_W), a_re=a_re, a_im=a_im,
                at_re=at_re, at_im=at_im)


TM_PROJ = 512
TQ_ATTN = 1024
RC_ATTN = 256
TQ_ATTN_RUNNING_MAX = 512
RET_CHUNK = 256
TM_S5 = 512
TM_MERGE = 512
SAMPLE_PAGES = 16
RET_SAMPLE_BLK = 8


def kernel(x_prompt, x_sample, cache_k, cache_v, state_ret, state_s5_re, state_s5_im, page_table, norm_gain, w_in, q_norm_gain, k_norm_gain, lam_q1, lam_k1, lam_q2, lam_k2, attn_out_gain, ret_out_gain, s5_lam_re, s5_lam_im, s5_log_dt, s5_b_re, s5_b_im, s5_c_re, s5_c_im, s5_d, w_glu, b_glu, w_br_attn, w_br_ret, w_br_s5, w_out):
    depth = w_in.shape[0]
    nb, seq, _ = x_prompt.shape
    nd, dseq, _ = x_sample.shape
    assert dseq == 1, "the sample group is a single-token step"
    n_pages = page_table.shape[1]
    past_len = n_pages * PAGE_SIZE
    tm_proj, tq, c_ret = min(TM_PROJ, seq), min(TQ_ATTN, seq), min(RET_CHUNK, seq)
    tm_s5, tm_merge = min(TM_S5, seq), min(TM_MERGE, seq)
    pages = min(SAMPLE_PAGES, n_pages)
    assert seq % tm_proj == 0 and seq % tq == 0 and seq % min(TQ_ATTN_RUNNING_MAX, seq) == 0 and seq % c_ret == 0 and seq % tm_s5 == 0 and seq % tm_merge == 0
    assert n_pages % pages == 0 and nd % RET_SAMPLE_BLK == 0

    tabs_p = _rot_tables(np.arange(seq), seq)
    tabs_s = _rot_tables(past_len + np.arange(1), nd)
    grp = (jnp.arange(LANES)[:, None] // A_DH == jnp.arange(LANES)[None, :] // A_DH).astype(BF16)

    yp = x_prompt.reshape(nb * seq, D_MODEL)
    ys = x_sample.reshape(nd, D_MODEL)
    outs = {k: [] for k in ('rp', 'spr', 'spi', 'rs', 'ssr', 'ssi')}
    w_all = w_in.astype(BF16)
    kv_p = kv_s = None
    for l in range(depth):
        lam_init = 0.8 - 0.6 * math.exp(-0.3 * l)
        p = dict(
            norm_gain=norm_gain[l].reshape(1, D_MODEL), grp=grp,
            q_gain=jnp.tile(q_norm_gain[l], LANES // A_DH).reshape(1, LANES),
            k_gain=jnp.tile(k_norm_gain[l], LANES // A_DH).reshape(1, LANES),
            lam_q1=lam_q1[l].reshape(1, A_DH), lam_k1=lam_k1[l].reshape(1, A_DH),
            lam_q2=lam_q2[l].reshape(1, A_DH), lam_k2=lam_k2[l].reshape(1, A_DH),
            attn_gain=attn_out_gain[l].reshape(1, BRANCH_W), ret_gain=ret_out_gain[l].reshape(1, BRANCH_W),
            w_glu=w_glu[l].astype(BF16), b_glu=b_glu[l].reshape(1, 2 * BRANCH_W),
            w_br_attn=w_br_attn[l].astype(BF16), w_br_ret=w_br_ret[l].astype(BF16),
            w_br_s5=w_br_s5[l].astype(BF16), w_out=w_out[l].astype(BF16))
        s5 = _s5_params(s5_lam_re[l], s5_lam_im[l], s5_log_dt[l], s5_b_re[l], s5_b_im[l], s5_c_re[l], s5_c_im[l],
                        s5_d[l], (tm_s5 // SUBLANES,))

        mix = _inproj(yp, p, w_all, tabs_p, tm_proj, seq, BF16, l, depth, kv_p)
        kv_p = (mix['kf'], mix['vf'])
        o_attn = _attn_prompt(mix, p, nb, seq, lam_init, tq, min(RC_ATTN, tq), min(TQ_ATTN_RUNNING_MAX, seq))
        o_ret, ret_state = _ret_prompt(mix, p, nb, seq, c_ret)
        y_s5, s_re, s_im = _s5_prompt(mix, s5, nb, seq, tm_s5)
        yp = _merge(yp, o_attn, o_ret, y_s5, mix['ssg'], p, w_all, l, tm_merge)
        outs['rp'].append(ret_state)
        outs['spr'].append(s_re)
        outs['spi'].append(s_im)

        mix = _inproj(ys, p, w_all, tabs_s, nd, nd, F32, l, depth, kv_s)
        kv_s = (mix['kf'], mix['vf'])
        o_attn = _attn_sample(mix, p, cache_k, cache_v, page_table, l, lam_init, pages)
        o_ret, ret_state = _ret_sample(mix, p, state_ret, l, RET_SAMPLE_BLK)
        y_s5, s_re, s_im = _s5_sample(mix, s5, state_s5_re[l], state_s5_im[l])
        ys = _merge(ys, o_attn, o_ret, y_s5, mix['ssg'], p, w_all, l, nd)
        outs['rs'].append(ret_state)
        outs['ssr'].append(s_re)
        outs['ssi'].append(s_im)

    st = lambda k: jnp.stack(outs[k])
    kv5 = lambda a, b, t: a.reshape(depth, b, t, A_HEADS, A_DV)
    return (yp.reshape(nb, seq, D_MODEL), ys.reshape(nd, 1, D_MODEL),
            kv5(kv_p[0], nb, seq), kv5(kv_p[1], nb, seq), st('rp'), st('spr'), st('spi'),
            kv5(kv_s[0], nd, 1), kv5(kv_s[1], nd, 1), st('rs'), st('ssr'), st('ssi'))
```
